```python
import math
import jax, jax.numpy as jnp
from jax import lax
import numpy as np

D_MODEL = 1024
BATCH = 8
SEQ = 4096
DEPTH = 2

N_META = 16
Q_BLOCK = 128
SB_HEADS = 8
SB_HEAD_DIM = 64
SB_WIDTH = SB_HEADS * SB_HEAD_DIM
MLA_HEADS = 8
MLA_NOPE = 64
MLA_ROPE = 32
MLA_V = 64
Q_LORA = 384
KV_LORA = 256
MLA_WIDTH = MLA_HEADS * MLA_V
MIX_WIDTH = SB_WIDTH + MLA_WIDTH
IN_COLS = 3 * SB_WIDTH + Q_LORA + KV_LORA + MLA_ROPE
D_FF = -(-8 * D_MODEL // (3 * 256)) * 256
ROPE_THETA = 10000.0
EPS = 1e-6

kernel_name = "hybrid_stickbreaking_mla_block"


def rms_norm(x, gain):
    xf = x.astype(jnp.float32)
    y = xf * lax.rsqrt(jnp.mean(xf * xf, axis=-1, keepdims=True) + EPS)
    return y.astype(x.dtype) * gain.astype(x.dtype)


def rope_tables(length):
    inv_freq = 1.0 / (ROPE_THETA ** (jnp.arange(0, MLA_ROPE, 2, dtype=jnp.float32) / MLA_ROPE))
    ang = jnp.arange(length, dtype=jnp.float32)[:, None] * inv_freq[None, :]
    return jnp.cos(ang), jnp.sin(ang)


def apply_rope(x, cos, sin):
    xf = x.astype(jnp.float32)
    half = MLA_ROPE // 2
    x1, x2 = xf[..., :half], xf[..., half:]
    c = cos[None, :, None, :]
    s = sin[None, :, None, :]
    return jnp.concatenate([x1 * c - x2 * s, x2 * c + x1 * s], axis=-1).astype(x.dtype)


def stick_breaking_block(q, k, v, q_start):
    scale = 1.0 / math.sqrt(SB_HEAD_DIM)
    z = jnp.einsum('bqhd,bkhd->bhqk', q.astype(jnp.float32), k.astype(jnp.float32)) * scale
    t_idx = q_start + jnp.arange(q.shape[1])
    s_idx = jnp.arange(k.shape[1])
    causal = s_idx[None, :] < t_idx[:, None]
    log_1m = jnp.where(causal, jax.nn.log_sigmoid(-z), 0.0)
    suffix = lax.cumsum(log_1m, axis=3, reverse=True) - log_1m
    a = jnp.where(causal, jnp.exp(jax.nn.log_sigmoid(z) + suffix), 0.0)
    o = jnp.einsum('bhqk,bkhd->bqhd', a, v.astype(jnp.float32))
    return o.astype(v.dtype)


def mla_block(q, k, v, q_start):
    scale = 1.0 / math.sqrt(MLA_NOPE + MLA_ROPE)
    s = jnp.einsum('bqhd,bkhd->bhqk', q.astype(jnp.float32), k.astype(jnp.float32)) * scale
    t_idx = q_start + jnp.arange(q.shape[1])
    s_idx = jnp.arange(k.shape[1])
    causal = s_idx[None, :] <= t_idx[:, None]
    s = jnp.where(causal, s, jnp.finfo(jnp.float32).min)
    p = jax.nn.softmax(s, axis=-1)
    o = jnp.einsum('bhqk,bkhd->bqhd', p, v.astype(jnp.float32))
    return o.astype(v.dtype)


def setup_inputs(seed: int = 0) -> dict:
    key = jax.random.key(seed)
    ks = jax.random.split(key, 20)
    f32 = jnp.float32

    def w(k, shape, fan_in):
        return jax.random.normal(k, shape, f32) * (fan_in ** -0.5)

    def gain(k, shape):
        return 1.0 + 0.05 * jax.random.normal(k, shape, f32)

    return {
        "x": jax.random.normal(ks[0], (BATCH, SEQ, D_MODEL), f32),
        "meta_tokens": jax.random.normal(ks[1], (N_META, D_MODEL), f32),
        "w_in": w(ks[2], (DEPTH, D_MODEL, IN_COLS), D_MODEL),
        "q_lat_norm": gain(ks[3], (DEPTH, Q_LORA)),
        "kv_lat_norm": gain(ks[4], (DEPTH, KV_LORA)),
        "w_uq": w(ks[5], (DEPTH, Q_LORA, MLA_HEADS * (MLA_NOPE + MLA_ROPE)), Q_LORA),
        "w_ukv": w(ks[6], (DEPTH, KV_LORA, MLA_HEADS * (MLA_NOPE + MLA_V)), KV_LORA),
        "sb_out_norm": gain(ks[7], (DEPTH, SB_WIDTH)),
        "mla_out_norm": gain(ks[8], (DEPTH, MLA_WIDTH)),
        "w_o": w(ks[9], (DEPTH, MIX_WIDTH, D_MODEL), MIX_WIDTH),
        "pre_mix_norm": gain(ks[10], (DEPTH, D_MODEL)),
        "post_mix_norm": gain(ks[11], (DEPTH, D_MODEL)),
        "pre_ffn_norm": gain(ks[12], (DEPTH, D_MODEL)),
        "post_ffn_norm": gain(ks[13], (DEPTH, D_MODEL)),
        "w_gate": w(ks[14], (DEPTH, D_MODEL, D_FF), D_MODEL),
        "w_up": w(ks[15], (DEPTH, D_MODEL, D_FF), D_MODEL),
        "w_down": w(ks[16], (DEPTH, D_FF, D_MODEL), D_FF),
    }


def reference(x, meta_tokens, w_in, q_lat_norm, kv_lat_norm, w_uq, w_ukv, sb_out_norm, mla_out_norm,
              w_o, pre_mix_norm, post_mix_norm, pre_ffn_norm, post_ffn_norm, w_gate, w_up, w_down):
    B = x.shape[0]
    n_real = x.shape[1]
    meta = jnp.broadcast_to(meta_tokens.astype(x.dtype)[None], (B, N_META, D_MODEL))
    h = jnp.concatenate([meta, x], axis=1)
    L = h.shape[1]
    cos, sin = rope_tables(L)
    blocks = [(0, N_META)] + [(N_META + i * Q_BLOCK, N_META + (i + 1) * Q_BLOCK)
                              for i in range(n_real // Q_BLOCK)]
    o1 = 3 * SB_WIDTH
    o2 = o1 + Q_LORA
    o3 = o2 + KV_LORA

    for layer in range(DEPTH):
        u = rms_norm(h, pre_mix_norm[layer])
        proj = jnp.einsum('bld,dc->blc', u, w_in[layer])
        q_sb = proj[..., 0:SB_WIDTH].reshape(B, L, SB_HEADS, SB_HEAD_DIM)
        k_sb = proj[..., SB_WIDTH:2 * SB_WIDTH].reshape(B, L, SB_HEADS, SB_HEAD_DIM)
        v_sb = proj[..., 2 * SB_WIDTH:o1].reshape(B, L, SB_HEADS, SB_HEAD_DIM)
        c_q = rms_norm(proj[..., o1:o2], q_lat_norm[layer])
        c_kv = rms_norm(proj[..., o2:o3], kv_lat_norm[layer])
        k_rope = proj[..., o3:].reshape(B, L, 1, MLA_ROPE)

        q_m = jnp.einsum('blr,rc->blc', c_q, w_uq[layer]).reshape(B, L, MLA_HEADS, MLA_NOPE + MLA_ROPE)
        kv_m = jnp.einsum('blr,rc->blc', c_kv, w_ukv[layer]).reshape(B, L, MLA_HEADS, MLA_NOPE + MLA_V)
        q_mla = jnp.concatenate([q_m[..., :MLA_NOPE], apply_rope(q_m[..., MLA_NOPE:], cos, sin)], axis=-1)
        k_rope_r = jnp.broadcast_to(apply_rope(k_rope, cos, sin), (B, L, MLA_HEADS, MLA_ROPE))
        k_mla = jnp.concatenate([kv_m[..., :MLA_NOPE], k_rope_r], axis=-1)
        v_mla = kv_m[..., MLA_NOPE:]

        o_sb = jnp.concatenate([stick_breaking_block(q_sb[:, s:e], k_sb[:, :e], v_sb[:, :e], s)
                                for (s, e) in blocks], axis=1)
        o_mla = jnp.concatenate([mla_block(q_mla[:, s:e], k_mla[:, :e], v_mla[:, :e], s)
                                 for (s, e) in blocks], axis=1)

        merged = jnp.concatenate([rms_norm(o_sb.reshape(B, L, SB_WIDTH), sb_out_norm[layer]),
                                  rms_norm(o_mla.reshape(B, L, MLA_WIDTH), mla_out_norm[layer])], axis=-1)
        mix = jnp.einsum('blc,cd->bld', merged, w_o[layer])
        h = h + rms_norm(mix, post_mix_norm[layer])

        f = rms_norm(h, pre_ffn_norm[layer])
        g = jnp.einsum('bld,df->blf', f, w_gate[layer])
        up = jnp.einsum('bld,df->blf', f, w_up[layer])
        ffn = jnp.einsum('blf,fd->bld', jax.nn.silu(g) * up, w_down[layer])
        h = h + rms_norm(ffn, post_ffn_norm[layer])

    return h[:, N_META:]
```

```python
import functools
import math

import jax
import jax.numpy as jnp
from jax import lax
from jax.experimental import pallas as pl
from jax.experimental.pallas import tpu as pltpu

D_MODEL = 1024
N_META = 16
SB_HEADS = 8
SB_HEAD_DIM = 64
SB_WIDTH = SB_HEADS * SB_HEAD_DIM
MLA_HEADS = 8
MLA_NOPE = 64
MLA_ROPE = 32
MLA_V = 64
Q_LORA = 384
KV_LORA = 256
MLA_WIDTH = MLA_HEADS * MLA_V
D_FF = 2816
ROPE_THETA = 10000.0
EPS = 1e-6

LANE = 128
MLA_SLOT = LANE
SEQ_TILE = 256
IN_ROWS = 544
POST_ROWS = 512
FF_CHUNK = 256
SB_KEYS = 128
NEG_BIG = -1e30

_NT = (((1,), (1,)), ((), ()))
_F32 = jnp.float32
_BF16 = jnp.bfloat16


def _rms(x, gain):
    ms = jnp.mean(x * x, axis=-1, keepdims=True)
    return x * lax.rsqrt(ms + EPS) * gain


def _dot(a, b):
    return jnp.dot(a, b, preferred_element_type=_F32)


def _rope(x, cos, sin_lo, sin_hi):
    return x * cos + pltpu.roll(x, LANE - MLA_ROPE // 2, 1) * sin_lo + pltpu.roll(x, MLA_ROPE // 2, 1) * sin_hi


def _in_proj_kernel(h_ref, tab_ref, g_pre_ref, w_sb_ref, w_lat_ref, g_q_ref, g_kv_ref, wq_ref, wk_ref, wv_ref,
                    qsb_ref, ksb_ref, vsb_ref, qm_ref, km_ref, vm_ref):
    u = _rms(h_ref[...], g_pre_ref[...]).astype(_BF16)
    p_sb = _dot(u, w_sb_ref[...])
    qsb_ref[...] = (p_sb[:, :SB_WIDTH] * (1.0 / math.sqrt(SB_HEAD_DIM))).astype(_BF16)
    ksb_ref[...] = p_sb[:, SB_WIDTH:2 * SB_WIDTH].astype(_BF16)
    vsb_ref[...] = p_sb[:, 2 * SB_WIDTH:].astype(_BF16)

    p_lat = _dot(u, w_lat_ref[...])
    c_q = _rms(p_lat[:, :Q_LORA], g_q_ref[...]).astype(_BF16)
    c_kv = _rms(p_lat[:, Q_LORA:Q_LORA + KV_LORA], g_kv_ref[...]).astype(_BF16)
    k_rope = p_lat[:, Q_LORA + KV_LORA:]

    tab = tab_ref[...]
    cos_q, slo_q, shi_q, cos_k, slo_k, shi_k = [tab[:, i * LANE:(i + 1) * LANE] for i in range(6)]
    k_rope = _rope(k_rope, cos_k, slo_k, shi_k)

    q = _dot(c_q, wq_ref[...])
    k_nope = _dot(c_kv, wk_ref[...])
    for hh in range(MLA_HEADS):
        sl = slice(hh * MLA_SLOT, (hh + 1) * MLA_SLOT)
        qm_ref[:, sl] = _rope(q[:, sl], cos_q, slo_q, shi_q).astype(_BF16)
        km_ref[:, sl] = (k_nope[:, sl] + k_rope).astype(_BF16)
    vm_ref[...] = _dot(c_kv, wv_ref[...]).astype(_BF16)


def _const_spec(shape):
    nd = len(shape)
    return pl.BlockSpec(shape, lambda *_: (0,) * nd, pipeline_mode=pl.Buffered(1))


def _in_proj(h2d, tab, g_pre, w_sb, w_lat, g_q, g_kv, wq, wk, wv, l_pad):
    m = h2d.shape[0]
    t = IN_ROWS
    tiles_per_seq = l_pad // t
    row = lambda w: pl.BlockSpec((t, w), lambda i: (i, 0))
    out_widths = (SB_WIDTH, SB_WIDTH, SB_WIDTH, MLA_HEADS * MLA_SLOT, MLA_HEADS * MLA_SLOT, MLA_WIDTH)
    return pl.pallas_call(
        _in_proj_kernel,
        grid=(m // t,),
        in_specs=[row(D_MODEL),
                  pl.BlockSpec((t, 6 * LANE), lambda i: (i % tiles_per_seq, 0)),
                  _const_spec(g_pre.shape), _const_spec(w_sb.shape), _const_spec(w_lat.shape),
                  _const_spec(g_q.shape), _const_spec(g_kv.shape),
                  _const_spec(wq.shape), _const_spec(wk.shape), _const_spec(wv.shape)],
        out_specs=[row(w) for w in out_widths],
        out_shape=[jax.ShapeDtypeStruct((m, w), _BF16) for w in out_widths],
        compiler_params=pltpu.CompilerParams(dimension_semantics=("arbitrary",),
                                             vmem_limit_bytes=52 * 1024 * 1024),
        name="in_proj",
    )(h2d, tab, g_pre, w_sb, w_lat, g_q, g_kv, wq, wk, wv)


def _sb_kernel(q_ref, k_ref, v_ref, tri_ref, o_ref, c_ref, acc_ref):
    tq, tk = SEQ_TILE, SB_KEYS
    qi = pl.program_id(2)
    q = q_ref[0]
    lane = lax.broadcasted_iota(jnp.int32, (tq, LANE), 1)
    row = lax.broadcasted_iota(jnp.int32, (tq, LANE), 0)
    low_head = lane < SB_HEAD_DIM
    zero = jnp.zeros_like(q)
    q_heads = (jnp.where(low_head, q, zero), jnp.where(low_head, zero, q))
    tri = tri_ref[...]
    c_ref[...] = jnp.zeros_like(c_ref)
    acc_ref[...] = jnp.zeros_like(acc_ref)

    def block(j, masked):
        start = pl.multiple_of(j * tk, tk)
        kb = k_ref[0, pl.ds(start, tk), :]
        vb = v_ref[0, pl.ds(start, tk), :]
        if masked:
            valid = (start + lane) < (qi * tq + row)
        for hh in range(2):
            z = lax.dot_general(q_heads[hh], kb, _NT, preferred_element_type=_F32)
            lm = -(jnp.maximum(z, 0.0) + jnp.log1p(jnp.exp(-jnp.abs(z))))
            if masked:
                lm = jnp.where(valid, lm, 0.0)
            hi = lm.astype(_BF16)
            lo = (lm - hi.astype(_F32)).astype(_BF16)
            cs = _dot(jnp.concatenate([hi, lo], axis=1), tri)
            a = jnp.exp(z + cs[:, :tk] + c_ref[hh])
            if masked:
                a = jnp.where(valid, a, 0.0)
            acc_ref[hh] += _dot(a.astype(_BF16), vb)
            c_ref[hh] += cs[:, tk:]

    per_tile = tq // tk
    for d in range(per_tile - 1, -1, -1):
        block(qi * per_tile + d, True)

    n_off = qi * per_tile

    def body(i, carry):
        block(n_off - 1 - i, False)
        return carry

    lax.fori_loop(0, n_off, body, 0)
    o_ref[0] = jnp.where(low_head, acc_ref[0], acc_ref[1])


def _sb_attention(q, k, v, tri):
    b, l_pad, _ = q.shape
    tq = SEQ_TILE
    pairs = SB_WIDTH // LANE
    return pl.pallas_call(
        _sb_kernel,
        grid=(b, pairs, l_pad // tq),
        in_specs=[pl.BlockSpec((1, tq, LANE), lambda bi, hp, qi: (bi, qi, hp)),
                  pl.BlockSpec((1, l_pad, LANE), lambda bi, hp, qi: (bi, 0, hp)),
                  pl.BlockSpec((1, l_pad, LANE), lambda bi, hp, qi: (bi, 0, hp)),
                  pl.BlockSpec(tri.shape, lambda bi, hp, qi: (0, 0))],
        out_specs=pl.BlockSpec((1, tq, LANE), lambda bi, hp, qi: (bi, qi, hp)),
        out_shape=jax.ShapeDtypeStruct((b, l_pad, SB_WIDTH), _F32),
        scratch_shapes=[pltpu.VMEM((2, tq, LANE), _F32), pltpu.VMEM((2, tq, LANE), _F32)],
        compiler_params=pltpu.CompilerParams(dimension_semantics=("arbitrary",) * 3,
                                             vmem_limit_bytes=32 * 1024 * 1024),
        name="sb_attention",
    )(q, k, v, tri)


def _mla_kernel(q_ref, k_ref, v_ref, o_ref, m_ref, acc_ref):
    tq = tk = SEQ_TILE
    qi = pl.program_id(2)
    lane = lax.broadcasted_iota(jnp.int32, (tk, LANE), 1)
    low_head = lane < MLA_V
    m_ref[...] = jnp.full_like(m_ref, NEG_BIG)
    acc_ref[...] = jnp.zeros_like(acc_ref)

    def block(j, masked):
        start = pl.multiple_of(j * tk, tk)
        kb = k_ref[0, pl.ds(start, tk), :]
        vb = v_ref[0, pl.ds(start, tk), :]
        one = jnp.ones_like(vb)
        v_heads = (jnp.where(low_head, vb, one), jnp.where(low_head, one, vb))
        if masked:
            col = lax.broadcasted_iota(jnp.int32, (tq, tk), 1)
            rw = lax.broadcasted_iota(jnp.int32, (tq, tk), 0)
            valid = col <= rw
        for hh in range(2):
            sl = slice(hh * MLA_SLOT, (hh + 1) * MLA_SLOT)
            s = lax.dot_general(q_ref[0, :, sl], kb[:, sl], _NT, preferred_element_type=_F32)
            if masked:
                s = jnp.where(valid, s, NEG_BIG)
            m_old = m_ref[hh]
            m_new = jnp.maximum(m_old, jnp.max(s, axis=1, keepdims=True))
            alpha = jnp.exp(m_old - m_new)
            p = jnp.exp(s - m_new)
            acc_ref[hh] = alpha * acc_ref[hh] + _dot(p.astype(_BF16), v_heads[hh])
            m_ref[hh] = m_new

    def body(j, carry):
        block(j, False)
        return carry

    lax.fori_loop(0, qi, body, 0)
    block(qi, True)

    acc0 = acc_ref[0]
    acc1 = acc_ref[1]
    o_ref[0] = jnp.where(low_head, acc0 / pltpu.roll(acc0, MLA_V, 1), acc1 / pltpu.roll(acc1, MLA_V, 1))


def _mla_attention(q, k, v):
    b, l_pad, _ = q.shape
    tq = SEQ_TILE
    pairs = MLA_WIDTH // LANE
    return pl.pallas_call(
        _mla_kernel,
        grid=(b, pairs, l_pad // tq),
        in_specs=[pl.BlockSpec((1, tq, 2 * MLA_SLOT), lambda bi, hp, qi: (bi, qi, hp)),
                  pl.BlockSpec((1, l_pad, 2 * MLA_SLOT), lambda bi, hp, qi: (bi, 0, hp)),
                  pl.BlockSpec((1, l_pad, LANE), lambda bi, hp, qi: (bi, 0, hp))],
        out_specs=pl.BlockSpec((1, tq, LANE), lambda bi, hp, qi: (bi, qi, hp)),
        out_shape=jax.ShapeDtypeStruct((b, l_pad, MLA_WIDTH), _F32),
        scratch_shapes=[pltpu.VMEM((2, tq, 1), _F32), pltpu.VMEM((2, tq, LANE), _F32)],
        compiler_params=pltpu.CompilerParams(dimension_semantics=("arbitrary",) * 3,
                                             vmem_limit_bytes=32 * 1024 * 1024),
        name="mla_attention",
    )(q, k, v)


def _post_kernel(h_ref, osb_ref, omla_ref, g_sb_ref, g_mla_ref, wo_ref, g_post_ref, g_ffn_ref,
                 wg_ref, wu_ref, wd_ref, g_out_ref, out_ref, f_ref, acc_ref):
    a = _rms(osb_ref[...], g_sb_ref[...]).astype(_BF16)
    b = _rms(omla_ref[...], g_mla_ref[...]).astype(_BF16)
    mix = _dot(a, wo_ref[:SB_WIDTH, :]) + _dot(b, wo_ref[SB_WIDTH:, :])
    h1 = h_ref[...] + _rms(mix, g_post_ref[...])
    out_ref[...] = h1
    f_ref[...] = _rms(h1, g_ffn_ref[...]).astype(_BF16)
    acc_ref[...] = jnp.zeros_like(acc_ref)

    def chunk(c, carry):
        f = f_ref[...]
        g = _dot(f, wg_ref[c])
        up = _dot(f, wu_ref[c])
        act = g * (1.0 / (1.0 + jnp.exp(-g))) * up
        acc_ref[...] += _dot(act.astype(_BF16), wd_ref[c])
        return carry

    lax.fori_loop(0, D_FF // FF_CHUNK, chunk, 0)
    out_ref[...] += _rms(acc_ref[...], g_out_ref[...])


def _post(h2d, o_sb, o_mla, g_sb, g_mla, wo, g_post, g_ffn, wg, wu, wd, g_out):
    m = h2d.shape[0]
    t = POST_ROWS
    row = lambda w: pl.BlockSpec((t, w), lambda i: (i, 0))
    consts = (g_sb, g_mla, wo, g_post, g_ffn, wg, wu, wd, g_out)
    return pl.pallas_call(
        _post_kernel,
        grid=(m // t,),
        in_specs=[row(D_MODEL), row(SB_WIDTH), row(MLA_WIDTH)] + [_const_spec(c.shape) for c in consts],
        out_specs=row(D_MODEL),
        out_shape=jax.ShapeDtypeStruct((m, D_MODEL), _F32),
        scratch_shapes=[pltpu.VMEM((t, D_MODEL), _BF16), pltpu.VMEM((t, D_MODEL), _F32)],
        compiler_params=pltpu.CompilerParams(dimension_semantics=("arbitrary",),
                                             vmem_limit_bytes=56 * 1024 * 1024),
        name="post_mix_ffn",
    )(h2d, o_sb, o_mla, *consts)


def _rope_table(l_pad):
    half = MLA_ROPE // 2
    inv_freq = 1.0 / (ROPE_THETA ** (jnp.arange(0, MLA_ROPE, 2, dtype=_F32) / MLA_ROPE))
    ang = jnp.arange(l_pad, dtype=_F32)[:, None] * inv_freq[None, :]
    cos, sin = jnp.cos(ang), jnp.sin(ang)
    z = lambda w: jnp.zeros((l_pad, w), _F32)
    tail = LANE - MLA_NOPE - MLA_ROPE
    cos_t = jnp.concatenate([jnp.ones((l_pad, MLA_NOPE), _F32), cos, cos, z(tail)], axis=1)
    sin_lo = jnp.concatenate([z(MLA_NOPE), -sin, z(half), z(tail)], axis=1)
    sin_hi = jnp.concatenate([z(MLA_NOPE), z(half), sin, z(tail)], axis=1)
    scale = 1.0 / math.sqrt(MLA_NOPE + MLA_ROPE)
    return jnp.concatenate([cos_t * scale, sin_lo * scale, sin_hi * scale, cos_t, sin_lo, sin_hi], axis=1)


def _triangular():
    j = jnp.arange(2 * SB_KEYS)[:, None] % SB_KEYS
    s = jnp.arange(2 * SB_KEYS)[None, :]
    return ((s >= SB_KEYS) | (j >= s)).astype(_BF16)


def kernel(x, meta_tokens, w_in, q_lat_norm, kv_lat_norm, w_uq, w_ukv, sb_out_norm, mla_out_norm, w_o,
           pre_mix_norm, post_mix_norm, pre_ffn_norm, post_ffn_norm, w_gate, w_up, w_down):
    b, seq, _ = x.shape
    depth = w_in.shape[0]
    l_real = N_META + seq
    l_pad = -(-l_real // SEQ_TILE) * SEQ_TILE
    assert l_pad % IN_ROWS == 0 and (b * l_pad) % POST_ROWS == 0
    meta = jnp.broadcast_to(meta_tokens.astype(x.dtype)[None], (b, N_META, D_MODEL))
    h = jnp.concatenate([meta, x, jnp.zeros((b, l_pad - l_real, D_MODEL), x.dtype)], axis=1)
    h = h.reshape(b * l_pad, D_MODEL)
    tab = _rope_table(l_pad)
    tri = _triangular()
    o1 = 3 * SB_WIDTH
    o2 = o1 + Q_LORA
    o3 = o2 + KV_LORA
    n_chunks = D_FF // FF_CHUNK
    gain = lambda g: g.reshape(1, -1).astype(_F32)

    for layer in range(depth):
        wl = w_in[layer]
        w_sb = wl[:, :o1].astype(_BF16)
        zc = lambda w: jnp.zeros((D_MODEL, w), wl.dtype)
        w_lat = jnp.concatenate([wl[:, o1:o3], zc(MLA_NOPE), wl[:, o3:], zc(LANE - MLA_NOPE - MLA_ROPE)],
                                axis=1).astype(_BF16)
        wq = w_uq[layer].reshape(Q_LORA, MLA_HEADS, MLA_NOPE + MLA_ROPE)
        wq = jnp.pad(wq, ((0, 0), (0, 0), (0, MLA_SLOT - MLA_NOPE - MLA_ROPE)))
        wq = wq.reshape(Q_LORA, MLA_HEADS * MLA_SLOT).astype(_BF16)
        wkv = w_ukv[layer].reshape(KV_LORA, MLA_HEADS, MLA_NOPE + MLA_V)
        wk = jnp.pad(wkv[:, :, :MLA_NOPE], ((0, 0), (0, 0), (0, MLA_SLOT - MLA_NOPE)))
        wk = wk.reshape(KV_LORA, MLA_HEADS * MLA_SLOT).astype(_BF16)
        wv = wkv[:, :, MLA_NOPE:].reshape(KV_LORA, MLA_WIDTH).astype(_BF16)

        q_sb, k_sb, v_sb, q_m, k_m, v_m = _in_proj(
            h, tab, gain(pre_mix_norm[layer]), w_sb, w_lat, gain(q_lat_norm[layer]), gain(kv_lat_norm[layer]),
            wq, wk, wv, l_pad)
        seq3 = lambda a: a.reshape(b, l_pad, a.shape[-1])
        o_sb = _sb_attention(seq3(q_sb), seq3(k_sb), seq3(v_sb), tri)
        o_mla = _mla_attention(seq3(q_m), seq3(k_m), seq3(v_m))

        wg = w_gate[layer].reshape(D_MODEL, n_chunks, FF_CHUNK).transpose(1, 0, 2).astype(_BF16)
        wu = w_up[layer].reshape(D_MODEL, n_chunks, FF_CHUNK).transpose(1, 0, 2).astype(_BF16)
        wd = w_down[layer].reshape(n_chunks, FF_CHUNK, D_MODEL).astype(_BF16)
        h = _post(h, o_sb.reshape(b * l_pad, SB_WIDTH), o_mla.reshape(b * l_pad, MLA_WIDTH),
                  gain(sb_out_norm[layer]), gain(mla_out_norm[layer]), w_o[layer].astype(_BF16),
                  gain(post_mix_norm[layer]), gain(pre_ffn_norm[layer]), wg, wu, wd, gain(post_ffn_norm[layer]))

    return h.reshape(b, l_pad, D_MODEL)[:, N_META:l_real]
```

```python
import functools
import math

import jax
import jax.numpy as jnp
from jax import lax
from jax.experimental import pallas as pl
from jax.experimental.pallas import tpu as pltpu

D_MODEL = 1024
N_META = 16
SB_HEADS = 8
SB_HEAD_DIM = 64
SB_WIDTH = SB_HEADS * SB_HEAD_DIM
MLA_HEADS = 8
MLA_NOPE = 64
MLA_ROPE = 32
MLA_V = 64
Q_LORA = 384
KV_LORA = 256
MLA_WIDTH = MLA_HEADS * MLA_V
D_FF = 2816
ROPE_THETA = 10000.0
EPS = 1e-6

LANE = 128
MLA_SLOT = LANE
SEQ_TILE = 256
IN_ROWS = 544
POST_ROWS = 512
FF_CHUNK = 256
SB_KEYS = 128
NEG_BIG = -1e30
LOG2E = math.log2(math.e)

_NT = (((1,), (1,)), ((), ()))
_F32 = jnp.float32
_BF16 = jnp.bfloat16


def _rms(x, gain):
    ms = jnp.mean(x * x, axis=-1, keepdims=True)
    return x * lax.rsqrt(ms + EPS) * gain


def _dot(a, b):
    return jnp.dot(a, b, preferred_element_type=_F32)


def _rope(x, cos, sin_lo, sin_hi):
    return x * cos + pltpu.roll(x, LANE - MLA_ROPE // 2, 1) * sin_lo + pltpu.roll(x, MLA_ROPE // 2, 1) * sin_hi


def _in_proj_kernel(h_ref, tab_ref, g_pre_ref, w_sb_ref, w_lat_ref, g_q_ref, g_kv_ref, wq_ref, wk_ref, wv_ref,
                    qsb_ref, ksb_ref, vsb_ref, qm_ref, km_ref, vm_ref):
    u = _rms(h_ref[...], g_pre_ref[...]).astype(_BF16)
    p_sb = _dot(u, w_sb_ref[...])
    qsb_ref[...] = (p_sb[:, :SB_WIDTH] * (-LOG2E / math.sqrt(SB_HEAD_DIM))).astype(_BF16)
    ksb_ref[...] = p_sb[:, SB_WIDTH:2 * SB_WIDTH].astype(_BF16)
    vsb_ref[...] = p_sb[:, 2 * SB_WIDTH:].astype(_BF16)

    p_lat = _dot(u, w_lat_ref[...])
    c_q = _rms(p_lat[:, :Q_LORA], g_q_ref[...]).astype(_BF16)
    c_kv = _rms(p_lat[:, Q_LORA:Q_LORA + KV_LORA], g_kv_ref[...]).astype(_BF16)
    k_rope = p_lat[:, Q_LORA + KV_LORA:]

    tab = tab_ref[...]
    cos_q, slo_q, shi_q, cos_k, slo_k, shi_k = [tab[:, i * LANE:(i + 1) * LANE] for i in range(6)]
    k_rope = _rope(k_rope, cos_k, slo_k, shi_k)

    q = _dot(c_q, wq_ref[...])
    k_nope = _dot(c_kv, wk_ref[...])
    for hh in range(MLA_HEADS):
        sl = slice(hh * MLA_SLOT, (hh + 1) * MLA_SLOT)
        qm_ref[:, sl] = _rope(q[:, sl], cos_q, slo_q, shi_q).astype(_BF16)
        km_ref[:, sl] = (k_nope[:, sl] + k_rope).astype(_BF16)
    vm_ref[...] = _dot(c_kv, wv_ref[...]).astype(_BF16)


def _const_spec(shape):
    nd = len(shape)
    return pl.BlockSpec(shape, lambda *_: (0,) * nd, pipeline_mode=pl.Buffered(1))


def _in_proj(h2d, tab, g_pre, w_sb, w_lat, g_q, g_kv, wq, wk, wv, l_pad):
    m = h2d.shape[0]
    t = IN_ROWS
    tiles_per_seq = l_pad // t
    row = lambda w: pl.BlockSpec((t, w), lambda i: (i, 0))
    out_widths = (SB_WIDTH, SB_WIDTH, SB_WIDTH, MLA_HEADS * MLA_SLOT, MLA_HEADS * MLA_SLOT, MLA_WIDTH)
    return pl.pallas_call(
        _in_proj_kernel,
        grid=(m // t,),
        in_specs=[row(D_MODEL),
                  pl.BlockSpec((t, 6 * LANE), lambda i: (i % tiles_per_seq, 0)),
                  _const_spec(g_pre.shape), _const_spec(w_sb.shape), _const_spec(w_lat.shape),
                  _const_spec(g_q.shape), _const_spec(g_kv.shape),
                  _const_spec(wq.shape), _const_spec(wk.shape), _const_spec(wv.shape)],
        out_specs=[row(w) for w in out_widths],
        out_shape=[jax.ShapeDtypeStruct((m, w), _BF16) for w in out_widths],
        compiler_params=pltpu.CompilerParams(dimension_semantics=("arbitrary",),
                                             vmem_limit_bytes=52 * 1024 * 1024),
        name="in_proj",
    )(h2d, tab, g_pre, w_sb, w_lat, g_q, g_kv, wq, wk, wv)


def _sb_kernel(q_ref, k_ref, vt_ref, tri_ref, o_ref, qh_ref, c_ref, acc_ref):
    tq, tk = SEQ_TILE, SB_KEYS
    qi = pl.program_id(1)
    low_head = lax.broadcasted_iota(jnp.int32, (tq, LANE), 1) < SB_HEAD_DIM
    for hp in range(SB_WIDTH // LANE):
        qp = q_ref[0, :, hp * LANE:(hp + 1) * LANE]
        zero = jnp.zeros_like(qp)
        qh_ref[2 * hp] = jnp.where(low_head, qp, zero)
        qh_ref[2 * hp + 1] = jnp.where(low_head, zero, qp)
    tri = tri_ref[...]
    c_ref[...] = jnp.zeros_like(c_ref)
    acc_ref[...] = jnp.zeros_like(acc_ref)

    def block(j, masked):
        start = pl.multiple_of(j * tk, tk)
        if masked:
            key = start + lax.broadcasted_iota(jnp.int32, (tk, tq), 0)
            qry = qi * tq + lax.broadcasted_iota(jnp.int32, (tk, tq), 1)
            valid = key < qry

        def scores(h):
            hp = h // 2
            kb = k_ref[0, pl.ds(start, tk), hp * LANE:(hp + 1) * LANE]
            return lax.dot_general(kb, qh_ref[h], _NT, preferred_element_type=_F32)

        def cumsum(zn):
            neg_abs = lax.bitcast_convert_type(lax.bitcast_convert_type(zn, jnp.uint32) | jnp.uint32(0x80000000), _F32)
            lm = jnp.minimum(zn, 0.0) - jnp.log2(1.0 + jnp.exp2(neg_abs))
            if masked:
                lm = jnp.where(valid, lm, 0.0)
            return _dot(tri, lm.astype(_BF16))

        def accumulate(h, zn, cs):
            a = jnp.exp2(cs + c_ref[h:h + 1, :] - zn)
            if masked:
                a = jnp.where(valid, a, 0.0)
            rows = slice(h * SB_HEAD_DIM, (h + 1) * SB_HEAD_DIM)
            acc_ref[rows, :] += _dot(vt_ref[0, j, rows, :], a.astype(_BF16))
            c_ref[h:h + 1, :] += cs[0:1, :]

        z = [None] * SB_HEADS
        cs = [None] * SB_HEADS
        for step in range(SB_HEADS + 2):
            if step < SB_HEADS:
                z[step] = scores(step)
            if 1 <= step <= SB_HEADS:
                cs[step - 1] = cumsum(z[step - 1])
            if step >= 2:
                accumulate(step - 2, z[step - 2], cs[step - 2])

    per_tile = tq // tk
    for d in range(per_tile - 1, -1, -1):
        block(qi * per_tile + d, True)

    n_off = qi * per_tile

    def body(i, carry):
        block(n_off - 1 - i, False)
        return carry

    lax.fori_loop(0, n_off, body, 0)
    for hp in range(SB_WIDTH // LANE):
        o_ref[0, :, hp * LANE:(hp + 1) * LANE] = acc_ref[hp * LANE:(hp + 1) * LANE, :].T


def _sb_attention(q, k, vt, tri):
    b, l_pad, _ = q.shape
    tq = SEQ_TILE
    return pl.pallas_call(
        _sb_kernel,
        grid=(b, l_pad // tq),
        in_specs=[pl.BlockSpec((1, tq, SB_WIDTH), lambda bi, qi: (bi, qi, 0)),
                  pl.BlockSpec((1, l_pad, SB_WIDTH), lambda bi, qi: (bi, 0, 0)),
                  pl.BlockSpec((1,) + vt.shape[1:], lambda bi, qi: (bi, 0, 0, 0)),
                  pl.BlockSpec(tri.shape, lambda bi, qi: (0, 0))],
        out_specs=pl.BlockSpec((1, tq, SB_WIDTH), lambda bi, qi: (bi, qi, 0)),
        out_shape=jax.ShapeDtypeStruct((b, l_pad, SB_WIDTH), _F32),
        scratch_shapes=[pltpu.VMEM((SB_HEADS, tq, LANE), _BF16),
                        pltpu.VMEM((SB_HEADS, tq), _F32),
                        pltpu.VMEM((SB_WIDTH, tq), _F32)],
        compiler_params=pltpu.CompilerParams(dimension_semantics=("arbitrary",) * 2,
                                             vmem_limit_bytes=40 * 1024 * 1024),
        name="sb_attention",
    )(q, k, vt, tri)


def _mla_kernel(q_ref, k_ref, vt_ref, o_ref, m_ref, l_ref, acc_ref):
    tq = tk = SEQ_TILE
    qi = pl.program_id(1)
    m_ref[...] = jnp.full_like(m_ref, NEG_BIG)
    l_ref[...] = jnp.zeros_like(l_ref)
    acc_ref[...] = jnp.zeros_like(acc_ref)

    def block(j, masked):
        start = pl.multiple_of(j * tk, tk)
        if masked:
            valid = lax.broadcasted_iota(jnp.int32, (tk, tq), 0) <= lax.broadcasted_iota(jnp.int32, (tk, tq), 1)

        def scores(h):
            sl = slice(h * MLA_SLOT, (h + 1) * MLA_SLOT)
            return lax.dot_general(k_ref[0, pl.ds(start, tk), sl], q_ref[0, :, sl], _NT,
                                   preferred_element_type=_F32)

        s_next = scores(0)
        for h in range(MLA_HEADS):
            s = s_next
            if h + 1 < MLA_HEADS:
                s_next = scores(h + 1)
            if masked:
                s = jnp.where(valid, s, NEG_BIG)
            m_old = m_ref[h:h + 1, :]
            m_new = jnp.maximum(m_old, jnp.max(s, axis=0, keepdims=True))
            alpha = jnp.exp2(m_old - m_new)
            p = jnp.exp2(s - m_new)
            l_ref[h:h + 1, :] = alpha * l_ref[h:h + 1, :] + jnp.sum(p, axis=0, keepdims=True)
            rows = slice(h * MLA_V, (h + 1) * MLA_V)
            acc_ref[rows, :] = alpha * acc_ref[rows, :] + _dot(vt_ref[0, j, rows, :], p.astype(_BF16))
            m_ref[h:h + 1, :] = m_new

    def body(j, carry):
        block(j, False)
        return carry

    lax.fori_loop(0, qi, body, 0)
    block(qi, True)

    for hp in range(MLA_WIDTH // LANE):
        pair = [acc_ref[h * MLA_V:(h + 1) * MLA_V, :] / l_ref[h:h + 1, :] for h in (2 * hp, 2 * hp + 1)]
        o_ref[0, :, hp * LANE:(hp + 1) * LANE] = jnp.concatenate(pair, axis=0).T


def _mla_attention(q, k, vt):
    b, l_pad, _ = q.shape
    tq = SEQ_TILE
    width = MLA_HEADS * MLA_SLOT
    return pl.pallas_call(
        _mla_kernel,
        grid=(b, l_pad // tq),
        in_specs=[pl.BlockSpec((1, tq, width), lambda bi, qi: (bi, qi, 0)),
                  pl.BlockSpec((1, l_pad, width), lambda bi, qi: (bi, 0, 0)),
                  pl.BlockSpec((1,) + vt.shape[1:], lambda bi, qi: (bi, 0, 0, 0))],
        out_specs=pl.BlockSpec((1, tq, MLA_WIDTH), lambda bi, qi: (bi, qi, 0)),
        out_shape=jax.ShapeDtypeStruct((b, l_pad, MLA_WIDTH), _F32),
        scratch_shapes=[pltpu.VMEM((MLA_HEADS, tq), _F32), pltpu.VMEM((MLA_HEADS, tq), _F32),
                        pltpu.VMEM((MLA_WIDTH, tq), _F32)],
        compiler_params=pltpu.CompilerParams(dimension_semantics=("arbitrary",) * 2,
                                             vmem_limit_bytes=48 * 1024 * 1024),
        name="mla_attention",
    )(q, k, vt)


def _post_kernel(h_ref, osb_ref, omla_ref, g_sb_ref, g_mla_ref, wo_ref, g_post_ref, g_ffn_ref,
                 wg_ref, wu_ref, wd_ref, g_out_ref, out_ref, f_ref, acc_ref):
    a = _rms(osb_ref[...], g_sb_ref[...]).astype(_BF16)
    b = _rms(omla_ref[...], g_mla_ref[...]).astype(_BF16)
    mix = _dot(a, wo_ref[:SB_WIDTH, :]) + _dot(b, wo_ref[SB_WIDTH:, :])
    h1 = h_ref[...] + _rms(mix, g_post_ref[...])
    out_ref[...] = h1
    f_ref[...] = _rms(h1, g_ffn_ref[...]).astype(_BF16)
    acc_ref[...] = jnp.zeros_like(acc_ref)

    def chunk(c, carry):
        f = f_ref[...]
        g = _dot(f, wg_ref[c])
        up = _dot(f, wu_ref[c])
        act = g * (1.0 / (1.0 + jnp.exp(-g))) * up
        acc_ref[...] += _dot(act.astype(_BF16), wd_ref[c])
        return carry

    lax.fori_loop(0, D_FF // FF_CHUNK, chunk, 0)
    out_ref[...] += _rms(acc_ref[...], g_out_ref[...])


def _post(h2d, o_sb, o_mla, g_sb, g_mla, wo, g_post, g_ffn, wg, wu, wd, g_out):
    m = h2d.shape[0]
    t = POST_ROWS
    row = lambda w: pl.BlockSpec((t, w), lambda i: (i, 0))
    consts = (g_sb, g_mla, wo, g_post, g_ffn, wg, wu, wd, g_out)
    return pl.pallas_call(
        _post_kernel,
        grid=(m // t,),
        in_specs=[row(D_MODEL), row(SB_WIDTH), row(MLA_WIDTH)] + [_const_spec(c.shape) for c in consts],
        out_specs=row(D_MODEL),
        out_shape=jax.ShapeDtypeStruct((m, D_MODEL), _F32),
        scratch_shapes=[pltpu.VMEM((t, D_MODEL), _BF16), pltpu.VMEM((t, D_MODEL), _F32)],
        compiler_params=pltpu.CompilerParams(dimension_semantics=("arbitrary",),
                                             vmem_limit_bytes=56 * 1024 * 1024),
        name="post_mix_ffn",
    )(h2d, o_sb, o_mla, *consts)


def _rope_table(l_pad):
    half = MLA_ROPE // 2
    inv_freq = 1.0 / (ROPE_THETA ** (jnp.arange(0, MLA_ROPE, 2, dtype=_F32) / MLA_ROPE))
    ang = jnp.arange(l_pad, dtype=_F32)[:, None] * inv_freq[None, :]
    cos, sin = jnp.cos(ang), jnp.sin(ang)
    z = lambda w: jnp.zeros((l_pad, w), _F32)
    tail = LANE - MLA_NOPE - MLA_ROPE
    cos_t = jnp.concatenate([jnp.ones((l_pad, MLA_NOPE), _F32), cos, cos, z(tail)], axis=1)
    sin_lo = jnp.concatenate([z(MLA_NOPE), -sin, z(half), z(tail)], axis=1)
    sin_hi = jnp.concatenate([z(MLA_NOPE), z(half), sin, z(tail)], axis=1)
    scale = LOG2E / math.sqrt(MLA_NOPE + MLA_ROPE)
    return jnp.concatenate([cos_t * scale, sin_lo * scale, sin_hi * scale, cos_t, sin_lo, sin_hi], axis=1)


def _triangular():
    s = jnp.arange(SB_KEYS)[:, None]
    j = jnp.arange(SB_KEYS)[None, :]
    return (j >= s).astype(_BF16)


def _key_blocks_transposed(v, b, l_pad, tk):
    return v.reshape(b, l_pad // tk, tk, v.shape[-1]).transpose(0, 1, 3, 2)


def kernel(x, meta_tokens, w_in, q_lat_norm, kv_lat_norm, w_uq, w_ukv, sb_out_norm, mla_out_norm, w_o,
           pre_mix_norm, post_mix_norm, pre_ffn_norm, post_ffn_norm, w_gate, w_up, w_down):
    b, seq, _ = x.shape
    depth = w_in.shape[0]
    l_real = N_META + seq
    l_pad = -(-l_real // SEQ_TILE) * SEQ_TILE
    assert l_pad % IN_ROWS == 0 and (b * l_pad) % POST_ROWS == 0
    meta = jnp.broadcast_to(meta_tokens.astype(x.dtype)[None], (b, N_META, D_MODEL))
    h = jnp.concatenate([meta, x, jnp.zeros((b, l_pad - l_real, D_MODEL), x.dtype)], axis=1)
    h = h.reshape(b * l_pad, D_MODEL)
    tab = _rope_table(l_pad)
    tri = _triangular()
    o1 = 3 * SB_WIDTH
    o2 = o1 + Q_LORA
    o3 = o2 + KV_LORA
    n_chunks = D_FF // FF_CHUNK
    gain = lambda g: g.reshape(1, -1).astype(_F32)

    for layer in range(depth):
        wl = w_in[layer]
        w_sb = wl[:, :o1].astype(_BF16)
        zc = lambda w: jnp.zeros((D_MODEL, w), wl.dtype)
        w_lat = jnp.concatenate([wl[:, o1:o3], zc(MLA_NOPE), wl[:, o3:], zc(LANE - MLA_NOPE - MLA_ROPE)],
                                axis=1).astype(_BF16)
        wq = w_uq[layer].reshape(Q_LORA, MLA_HEADS, MLA_NOPE + MLA_ROPE)
        wq = jnp.pad(wq, ((0, 0), (0, 0), (0, MLA_SLOT - MLA_NOPE - MLA_ROPE)))
        wq = wq.reshape(Q_LORA, MLA_HEADS * MLA_SLOT).astype(_BF16)
        wkv = w_ukv[layer].reshape(KV_LORA, MLA_HEADS, MLA_NOPE + MLA_V)
        wk = jnp.pad(wkv[:, :, :MLA_NOPE], ((0, 0), (0, 0), (0, MLA_SLOT - MLA_NOPE)))
        wk = wk.reshape(KV_LORA, MLA_HEADS * MLA_SLOT).astype(_BF16)
        wv = wkv[:, :, MLA_NOPE:].reshape(KV_LORA, MLA_WIDTH).astype(_BF16)

        q_sb, k_sb, v_sb, q_m, k_m, v_m = _in_proj(
            h, tab, gain(pre_mix_norm[layer]), w_sb, w_lat, gain(q_lat_norm[layer]), gain(kv_lat_norm[layer]),
            wq, wk, wv, l_pad)
        seq3 = lambda a: a.reshape(b, l_pad, a.shape[-1])
        o_sb = _sb_attention(seq3(q_sb), seq3(k_sb), _key_blocks_transposed(v_sb, b, l_pad, SB_KEYS), tri)
        o_mla = _mla_attention(seq3(q_m), seq3(k_m), _key_blocks_transposed(v_m, b, l_pad, SEQ_TILE))

        wg = w_gate[layer].reshape(D_MODEL, n_chunks, FF_CHUNK).transpose(1, 0, 2).astype(_BF16)
        wu = w_up[layer].reshape(D_MODEL, n_chunks, FF_CHUNK).transpose(1, 0, 2).astype(_BF16)
        wd = w_down[layer].reshape(n_chunks, FF_CHUNK, D_MODEL).astype(_BF16)
        h = _post(h, o_sb.reshape(b * l_pad, SB_WIDTH), o_mla.reshape(b * l_pad, MLA_WIDTH),
                  gain(sb_out_norm[layer]), gain(mla_out_norm[layer]), w_o[layer].astype(_BF16),
                  gain(post_mix_norm[layer]), gain(pre_ffn_norm[layer]), wg, wu, wd, gain(post_ffn_norm[layer]))

    return h.reshape(b, l_pad, D_MODEL)[:, N_META:l_real]
```

```python
import functools
import math

import jax
import jax.numpy as jnp
from jax import lax
from jax.experimental import pallas as pl
from jax.experimental.pallas import tpu as pltpu

D_MODEL = 1024
N_META = 16
SB_HEADS = 8
SB_HEAD_DIM = 64
SB_WIDTH = SB_HEADS * SB_HEAD_DIM
MLA_HEADS = 8
MLA_NOPE = 64
MLA_ROPE = 32
MLA_V = 64
Q_LORA = 384
KV_LORA = 256
MLA_WIDTH = MLA_HEADS * MLA_V
D_FF = 2816
ROPE_THETA = 10000.0
EPS = 1e-6

LANE = 128
MLA_SLOT = LANE
SEQ_TILE = 256
IN_ROWS = 544
POST_ROWS = 512
FF_CHUNK = 256
SB_KEYS = 128
SB_GROUP = 4
MLA_GROUP = 4
NEG_BIG = -1e30
LOG2E = math.log2(math.e)
UNDERFLOW_LOG2 = -150.0

_NT = (((1,), (1,)), ((), ()))
_F32 = jnp.float32
_BF16 = jnp.bfloat16


def _rms(x, gain):
    ms = jnp.mean(x * x, axis=-1, keepdims=True)
    return x * lax.rsqrt(ms + EPS) * gain


def _dot(a, b):
    return jnp.dot(a, b, preferred_element_type=_F32)


def _rope(x, cos, sin_lo, sin_hi):
    return x * cos + pltpu.roll(x, LANE - MLA_ROPE // 2, 1) * sin_lo + pltpu.roll(x, MLA_ROPE // 2, 1) * sin_hi


def _in_proj_kernel(h_ref, tab_ref, g_pre_ref, w_sb_ref, w_lat_ref, g_q_ref, g_kv_ref, wq_ref, wk_ref, wv_ref,
                    qsb_ref, ksb_ref, vsb_ref, qm_ref, km_ref, vm_ref):
    u = _rms(h_ref[...], g_pre_ref[...]).astype(_BF16)
    p_sb = _dot(u, w_sb_ref[...])
    qsb_ref[...] = (p_sb[:, :SB_WIDTH] * (-LOG2E / math.sqrt(SB_HEAD_DIM))).astype(_BF16)
    ksb_ref[...] = p_sb[:, SB_WIDTH:2 * SB_WIDTH].astype(_BF16)
    vsb_ref[...] = p_sb[:, 2 * SB_WIDTH:].astype(_BF16)

    p_lat = _dot(u, w_lat_ref[...])
    c_q = _rms(p_lat[:, :Q_LORA], g_q_ref[...]).astype(_BF16)
    c_kv = _rms(p_lat[:, Q_LORA:Q_LORA + KV_LORA], g_kv_ref[...]).astype(_BF16)
    k_rope = p_lat[:, Q_LORA + KV_LORA:]

    tab = tab_ref[...]
    cos_q, slo_q, shi_q, cos_k, slo_k, shi_k = [tab[:, i * LANE:(i + 1) * LANE] for i in range(6)]
    k_rope = _rope(k_rope, cos_k, slo_k, shi_k)

    q = _dot(c_q, wq_ref[...])
    k_nope = _dot(c_kv, wk_ref[...])
    for hh in range(MLA_HEADS):
        sl = slice(hh * MLA_SLOT, (hh + 1) * MLA_SLOT)
        qm_ref[:, sl] = _rope(q[:, sl], cos_q, slo_q, shi_q).astype(_BF16)
        km_ref[:, sl] = (k_nope[:, sl] + k_rope).astype(_BF16)
    vm_ref[...] = _dot(c_kv, wv_ref[...]).astype(_BF16)


def _const_spec(shape):
    nd = len(shape)
    return pl.BlockSpec(shape, lambda *_: (0,) * nd, pipeline_mode=pl.Buffered(1))


def _in_proj(h2d, tab, g_pre, w_sb, w_lat, g_q, g_kv, wq, wk, wv, l_pad):
    m = h2d.shape[0]
    t = IN_ROWS
    tiles_per_seq = l_pad // t
    row = lambda w: pl.BlockSpec((t, w), lambda i: (i, 0))
    out_widths = (SB_WIDTH, SB_WIDTH, SB_WIDTH, MLA_HEADS * MLA_SLOT, MLA_HEADS * MLA_SLOT, MLA_WIDTH)
    return pl.pallas_call(
        _in_proj_kernel,
        grid=(m // t,),
        in_specs=[row(D_MODEL),
                  pl.BlockSpec((t, 6 * LANE), lambda i: (i % tiles_per_seq, 0)),
                  _const_spec(g_pre.shape), _const_spec(w_sb.shape), _const_spec(w_lat.shape),
                  _const_spec(g_q.shape), _const_spec(g_kv.shape),
                  _const_spec(wq.shape), _const_spec(wk.shape), _const_spec(wv.shape)],
        out_specs=[row(w) for w in out_widths],
        out_shape=[jax.ShapeDtypeStruct((m, w), _BF16) for w in out_widths],
        compiler_params=pltpu.CompilerParams(dimension_semantics=("arbitrary",),
                                             vmem_limit_bytes=52 * 1024 * 1024),
        name="in_proj",
    )(h2d, tab, g_pre, w_sb, w_lat, g_q, g_kv, wq, wk, wv)


def _sb_kernel(q_ref, k_ref, vt_ref, tri_ref, o_ref, qh_ref, c_ref, acc_ref):
    tq, tk = SEQ_TILE, SB_KEYS
    qi = pl.program_id(1)
    low_head = lax.broadcasted_iota(jnp.int32, (tq, LANE), 1) < SB_HEAD_DIM
    for hp in range(SB_WIDTH // LANE):
        qp = q_ref[0, :, hp * LANE:(hp + 1) * LANE]
        zero = jnp.zeros_like(qp)
        qh_ref[2 * hp] = jnp.where(low_head, qp, zero)
        qh_ref[2 * hp + 1] = jnp.where(low_head, zero, qp)
    tri = tri_ref[...]
    c_ref[...] = jnp.zeros_like(c_ref)
    acc_ref[...] = jnp.zeros_like(acc_ref)

    def block(j, masked):
        start = pl.multiple_of(j * tk, tk)
        if masked:
            key = start + lax.broadcasted_iota(jnp.int32, (tk, tq), 0)
            qry = qi * tq + lax.broadcasted_iota(jnp.int32, (tk, tq), 1)
            valid = key < qry

        def scores(h):
            hp = h // 2
            kb = k_ref[0, pl.ds(start, tk), hp * LANE:(hp + 1) * LANE]
            return lax.dot_general(kb, qh_ref[h], _NT, preferred_element_type=_F32)

        def cumsum(zn):
            low = jnp.minimum(zn, 0.0)
            lm = low - jnp.log2(1.0 + jnp.exp2((low - zn) + low))
            ls = lm - zn
            if masked:
                lm = jnp.where(valid, lm, 0.0)
            lm = lm.astype(_BF16)
            cs = _dot(tri, lm)
            return ls, cs, cs[0:1, :] + lm[0:1, :].astype(_F32)

        def accumulate(h, ls, cs, total):
            a = jnp.exp2(ls + cs + c_ref[h:h + 1, :])
            if masked:
                a = jnp.where(valid, a, 0.0)
            rows = slice(h * SB_HEAD_DIM, (h + 1) * SB_HEAD_DIM)
            acc_ref[rows, :] += _dot(vt_ref[0, j, rows, :], a.astype(_BF16))
            c_ref[h:h + 1, :] += total

        groups = [range(g, g + SB_GROUP) for g in range(0, SB_HEADS, SB_GROUP)]
        zn = {h: scores(h) for h in groups[0]}
        st = {}
        for gi, group in enumerate(groups):
            if gi + 1 < len(groups):
                zn.update({h: scores(h) for h in groups[gi + 1]})
            st.update({h: cumsum(zn[h]) for h in group})
            if gi >= 1:
                for h in groups[gi - 1]:
                    accumulate(h, *st[h])
        for h in groups[-1]:
            accumulate(h, *st[h])

    per_tile = tq // tk
    for d in range(per_tile - 1, -1, -1):
        block(qi * per_tile + d, True)

    n_off = qi * per_tile

    def more(carry):
        i, live = carry
        return jnp.logical_and(i < n_off, live > 0)

    def body(carry):
        i, _ = carry
        block(n_off - 1 - i, False)
        return i + 1, (jnp.max(c_ref[...]) > UNDERFLOW_LOG2).astype(jnp.int32)

    lax.while_loop(more, body, (jnp.int32(0), jnp.int32(1)))
    for hp in range(SB_WIDTH // LANE):
        o_ref[0, :, hp * LANE:(hp + 1) * LANE] = acc_ref[hp * LANE:(hp + 1) * LANE, :].T


def _sb_attention(q, k, vt, tri):
    b, l_pad, _ = q.shape
    tq = SEQ_TILE
    return pl.pallas_call(
        _sb_kernel,
        grid=(b, l_pad // tq),
        in_specs=[pl.BlockSpec((1, tq, SB_WIDTH), lambda bi, qi: (bi, qi, 0)),
                  pl.BlockSpec((1, l_pad, SB_WIDTH), lambda bi, qi: (bi, 0, 0)),
                  pl.BlockSpec((1,) + vt.shape[1:], lambda bi, qi: (bi, 0, 0, 0)),
                  pl.BlockSpec(tri.shape, lambda bi, qi: (0, 0))],
        out_specs=pl.BlockSpec((1, tq, SB_WIDTH), lambda bi, qi: (bi, qi, 0)),
        out_shape=jax.ShapeDtypeStruct((b, l_pad, SB_WIDTH), _F32),
        scratch_shapes=[pltpu.VMEM((SB_HEADS, tq, LANE), _BF16),
                        pltpu.VMEM((SB_HEADS, tq), _F32),
                        pltpu.VMEM((SB_WIDTH, tq), _F32)],
        compiler_params=pltpu.CompilerParams(dimension_semantics=("arbitrary",) * 2,
                                             vmem_limit_bytes=40 * 1024 * 1024),
        name="sb_attention",
    )(q, k, vt, tri)


def _mla_kernel(q_ref, k_ref, vt_ref, o_ref, s_ref, m_ref, l_ref, acc_ref):
    tq = tk = SEQ_TILE
    qi = pl.program_id(1)
    m_ref[...] = jnp.full_like(m_ref, NEG_BIG)
    l_ref[...] = jnp.zeros_like(l_ref)
    acc_ref[...] = jnp.zeros_like(acc_ref)
    lead = range(MLA_GROUP)
    rest = range(MLA_GROUP, MLA_HEADS)

    def scores(j, h):
        sl = slice(h * MLA_SLOT, (h + 1) * MLA_SLOT)
        start = pl.multiple_of(j * tk, tk)
        return lax.dot_general(k_ref[0, pl.ds(start, tk), sl], q_ref[0, :, sl], _NT,
                               preferred_element_type=_F32)

    def lead_scores(j):
        for h in lead:
            s_ref[h] = scores(j, h)

    def block(j, j_next, masked):
        if masked:
            valid = lax.broadcasted_iota(jnp.int32, (tk, tq), 0) <= lax.broadcasted_iota(jnp.int32, (tk, tq), 1)

        def softmax(h, s):
            if masked:
                s = jnp.where(valid, s, NEG_BIG)
            m_old = m_ref[h:h + 1, :]
            m_new = jnp.maximum(m_old, jnp.max(s, axis=0, keepdims=True))
            alpha = jnp.exp2(m_old - m_new)
            p = jnp.exp2(s - m_new)
            l_ref[h:h + 1, :] = alpha * l_ref[h:h + 1, :] + jnp.sum(p, axis=0, keepdims=True)
            m_ref[h:h + 1, :] = m_new
            return alpha, p.astype(_BF16)

        def weighted_values(h, alpha, p):
            rows = slice(h * MLA_V, (h + 1) * MLA_V)
            acc_ref[rows, :] = alpha * acc_ref[rows, :] + _dot(vt_ref[0, j, rows, :], p)

        s_rest = [scores(j, h) for h in rest]
        probs = [softmax(h, s_ref[h]) for h in lead]
        for h, pr in zip(lead, probs):
            weighted_values(h, *pr)
        if j_next is not None:
            lead_scores(j_next)
        probs = [softmax(h, s) for h, s in zip(rest, s_rest)]
        for h, pr in zip(rest, probs):
            weighted_values(h, *pr)

    def body(j, carry):
        block(j, j + 1, False)
        return carry

    lead_scores(0)
    lax.fori_loop(0, qi, body, 0)
    block(qi, None, True)

    for hp in range(MLA_WIDTH // LANE):
        pair = [acc_ref[h * MLA_V:(h + 1) * MLA_V, :] / l_ref[h:h + 1, :] for h in (2 * hp, 2 * hp + 1)]
        o_ref[0, :, hp * LANE:(hp + 1) * LANE] = jnp.concatenate(pair, axis=0).T


def _mla_attention(q, k, vt):
    b, l_pad, _ = q.shape
    tq = SEQ_TILE
    width = MLA_HEADS * MLA_SLOT
    return pl.pallas_call(
        _mla_kernel,
        grid=(b, l_pad // tq),
        in_specs=[pl.BlockSpec((1, tq, width), lambda bi, qi: (bi, qi, 0)),
                  pl.BlockSpec((1, l_pad, width), lambda bi, qi: (bi, 0, 0)),
                  pl.BlockSpec((1,) + vt.shape[1:], lambda bi, qi: (bi, 0, 0, 0))],
        out_specs=pl.BlockSpec((1, tq, MLA_WIDTH), lambda bi, qi: (bi, qi, 0)),
        out_shape=jax.ShapeDtypeStruct((b, l_pad, MLA_WIDTH), _F32),
        scratch_shapes=[pltpu.VMEM((MLA_GROUP, tq, tq), _F32),
                        pltpu.VMEM((MLA_HEADS, tq), _F32), pltpu.VMEM((MLA_HEADS, tq), _F32),
                        pltpu.VMEM((MLA_WIDTH, tq), _F32)],
        compiler_params=pltpu.CompilerParams(dimension_semantics=("arbitrary",) * 2,
                                             vmem_limit_bytes=48 * 1024 * 1024),
        name="mla_attention",
    )(q, k, vt)


def _post_kernel(h_ref, osb_ref, omla_ref, g_sb_ref, g_mla_ref, wo_ref, g_post_ref, g_ffn_ref,
                 wg_ref, wu_ref, wd_ref, g_out_ref, out_ref, f_ref, acc_ref):
    a = _rms(osb_ref[...], g_sb_ref[...]).astype(_BF16)
    b = _rms(omla_ref[...], g_mla_ref[...]).astype(_BF16)
    mix = _dot(a, wo_ref[:SB_WIDTH, :]) + _dot(b, wo_ref[SB_WIDTH:, :])
    h1 = h_ref[...] + _rms(mix, g_post_ref[...])
    out_ref[...] = h1
    f_ref[...] = _rms(h1, g_ffn_ref[...]).astype(_BF16)
    acc_ref[...] = jnp.zeros_like(acc_ref)

    def chunk(c, carry):
        f = f_ref[...]
        g = _dot(f, wg_ref[c])
        up = _dot(f, wu_ref[c])
        act = g * (1.0 / (1.0 + jnp.exp(-g))) * up
        acc_ref[...] += _dot(act.astype(_BF16), wd_ref[c])
        return carry

    lax.fori_loop(0, D_FF // FF_CHUNK, chunk, 0)
    out_ref[...] += _rms(acc_ref[...], g_out_ref[...])


def _post(h2d, o_sb, o_mla, g_sb, g_mla, wo, g_post, g_ffn, wg, wu, wd, g_out):
    m = h2d.shape[0]
    t = POST_ROWS
    row = lambda w: pl.BlockSpec((t, w), lambda i: (i, 0))
    consts = (g_sb, g_mla, wo, g_post, g_ffn, wg, wu, wd, g_out)
    return pl.pallas_call(
        _post_kernel,
        grid=(m // t,),
        in_specs=[row(D_MODEL), row(SB_WIDTH), row(MLA_WIDTH)] + [_const_spec(c.shape) for c in consts],
        out_specs=row(D_MODEL),
        out_shape=jax.ShapeDtypeStruct((m, D_MODEL), _F32),
        scratch_shapes=[pltpu.VMEM((t, D_MODEL), _BF16), pltpu.VMEM((t, D_MODEL), _F32)],
        compiler_params=pltpu.CompilerParams(dimension_semantics=("arbitrary",),
                                             vmem_limit_bytes=56 * 1024 * 1024),
        name="post_mix_ffn",
    )(h2d, o_sb, o_mla, *consts)


def _rope_table(l_pad):
    half = MLA_ROPE // 2
    inv_freq = 1.0 / (ROPE_THETA ** (jnp.arange(0, MLA_ROPE, 2, dtype=_F32) / MLA_ROPE))
    ang = jnp.arange(l_pad, dtype=_F32)[:, None] * inv_freq[None, :]
    cos, sin = jnp.cos(ang), jnp.sin(ang)
    z = lambda w: jnp.zeros((l_pad, w), _F32)
    tail = LANE - MLA_NOPE - MLA_ROPE
    cos_t = jnp.concatenate([jnp.ones((l_pad, MLA_NOPE), _F32), cos, cos, z(tail)], axis=1)
    sin_lo = jnp.concatenate([z(MLA_NOPE), -sin, z(half), z(tail)], axis=1)
    sin_hi = jnp.concatenate([z(MLA_NOPE), z(half), sin, z(tail)], axis=1)
    scale = LOG2E / math.sqrt(MLA_NOPE + MLA_ROPE)
    return jnp.concatenate([cos_t * scale, sin_lo * scale, sin_hi * scale, cos_t, sin_lo, sin_hi], axis=1)


def _triangular():
    s = jnp.arange(SB_KEYS)[:, None]
    j = jnp.arange(SB_KEYS)[None, :]
    return (j > s).astype(_BF16)


def _key_blocks_transposed(v, b, l_pad, tk):
    return v.reshape(b, l_pad // tk, tk, v.shape[-1]).transpose(0, 1, 3, 2)


def kernel(x, meta_tokens, w_in, q_lat_norm, kv_lat_norm, w_uq, w_ukv, sb_out_norm, mla_out_norm, w_o,
           pre_mix_norm, post_mix_norm, pre_ffn_norm, post_ffn_norm, w_gate, w_up, w_down):
    b, seq, _ = x.shape
    depth = w_in.shape[0]
    l_real = N_META + seq
    l_pad = -(-l_real // SEQ_TILE) * SEQ_TILE
    assert l_pad % IN_ROWS == 0 and (b * l_pad) % POST_ROWS == 0
    meta = jnp.broadcast_to(meta_tokens.astype(x.dtype)[None], (b, N_META, D_MODEL))
    h = jnp.concatenate([meta, x, jnp.zeros((b, l_pad - l_real, D_MODEL), x.dtype)], axis=1)
    h = h.reshape(b * l_pad, D_MODEL)
    tab = _rope_table(l_pad)
    tri = _triangular()
    o1 = 3 * SB_WIDTH
    o2 = o1 + Q_LORA
    o3 = o2 + KV_LORA
    n_chunks = D_FF // FF_CHUNK
    gain = lambda g: g.reshape(1, -1).astype(_F32)

    for layer in range(depth):
        wl = w_in[layer]
        w_sb = wl[:, :o1].astype(_BF16)
        zc = lambda w: jnp.zeros((D_MODEL, w), wl.dtype)
        w_lat = jnp.concatenate([wl[:, o1:o3], zc(MLA_NOPE), wl[:, o3:], zc(LANE - MLA_NOPE - MLA_ROPE)],
                                axis=1).astype(_BF16)
        wq = w_uq[layer].reshape(Q_LORA, MLA_HEADS, MLA_NOPE + MLA_ROPE)
        wq = jnp.pad(wq, ((0, 0), (0, 0), (0, MLA_SLOT - MLA_NOPE - MLA_ROPE)))
        wq = wq.reshape(Q_LORA, MLA_HEADS * MLA_SLOT).astype(_BF16)
        wkv = w_ukv[layer].reshape(KV_LORA, MLA_HEADS, MLA_NOPE + MLA_V)
        wk = jnp.pad(wkv[:, :, :MLA_NOPE], ((0, 0), (0, 0), (0, MLA_SLOT - MLA_NOPE)))
        wk = wk.reshape(KV_LORA, MLA_HEADS * MLA_SLOT).astype(_BF16)
        wv = wkv[:, :, MLA_NOPE:].reshape(KV_LORA, MLA_WIDTH).astype(_BF16)

        q_sb, k_sb, v_sb, q_m, k_m, v_m = _in_proj(
            h, tab, gain(pre_mix_norm[layer]), w_sb, w_lat, gain(q_lat_norm[layer]), gain(kv_lat_norm[layer]),
            wq, wk, wv, l_pad)
        seq3 = lambda a: a.reshape(b, l_pad, a.shape[-1])
        o_sb = _sb_attention(seq3(q_sb), seq3(k_sb), _key_blocks_transposed(v_sb, b, l_pad, SB_KEYS), tri)
        o_mla = _mla_attention(seq3(q_m), seq3(k_m), _key_blocks_transposed(v_m, b, l_pad, SEQ_TILE))

        wg = w_gate[layer].reshape(D_MODEL, n_chunks, FF_CHUNK).transpose(1, 0, 2).astype(_BF16)
        wu = w_up[layer].reshape(D_MODEL, n_chunks, FF_CHUNK).transpose(1, 0, 2).astype(_BF16)
        wd = w_down[layer].reshape(n_chunks, FF_CHUNK, D_MODEL).astype(_BF16)
        h = _post(h, o_sb.reshape(b * l_pad, SB_WIDTH), o_mla.reshape(b * l_pad, MLA_WIDTH),
                  gain(sb_out_norm[layer]), gain(mla_out_norm[layer]), w_o[layer].astype(_BF16),
                  gain(post_mix_norm[layer]), gain(pre_ffn_norm[layer]), wg, wu, wd, gain(post_ffn_norm[layer]))

    return h.reshape(b, l_pad, D_MODEL)[:, N_META:l_real]
```

```python
import functools
import math

import jax
import jax.numpy as jnp
from jax import lax
from jax.experimental import pallas as pl
from jax.experimental.pallas import tpu as pltpu

D_MODEL = 1024
N_META = 16
SB_HEADS = 8
SB_HEAD_DIM = 64
SB_WIDTH = SB_HEADS * SB_HEAD_DIM
MLA_HEADS = 8
MLA_NOPE = 64
MLA_ROPE = 32
MLA_V = 64
Q_LORA = 384
KV_LORA = 256
MLA_WIDTH = MLA_HEADS * MLA_V
D_FF = 2816
ROPE_THETA = 10000.0
EPS = 1e-6

LANE = 128
MLA_SLOT = LANE
SEQ_TILE = 256
IN_ROWS = 544
POST_ROWS = 512
FF_CHUNK = 256
SB_KEYS = 128
MLA_VROWS = MLA_V + 16
SB_GROUP = 4
MLA_GROUP = 4
NEG_BIG = -1e30
LOG2E = math.log2(math.e)
UNDERFLOW_LOG2 = -150.0

_NT = (((1,), (1,)), ((), ()))
_F32 = jnp.float32
_BF16 = jnp.bfloat16


def _rms(x, gain):
    ms = jnp.mean(x * x, axis=-1, keepdims=True)
    return x * lax.rsqrt(ms + EPS) * gain


def _dot(a, b):
    return jnp.dot(a, b, preferred_element_type=_F32)


def _rope(x, cos, sin_lo, sin_hi):
    return x * cos + pltpu.roll(x, LANE - MLA_ROPE // 2, 1) * sin_lo + pltpu.roll(x, MLA_ROPE // 2, 1) * sin_hi


def _in_proj_kernel(h_ref, tab_ref, g_pre_ref, w_sb_ref, w_lat_ref, g_q_ref, g_kv_ref, wq_ref, wk_ref, wv_ref,
                    qsb_ref, ksb_ref, vsb_ref, qm_ref, km_ref, vm_ref):
    u = _rms(h_ref[...], g_pre_ref[...]).astype(_BF16)
    p_sb = _dot(u, w_sb_ref[...])
    qsb_ref[...] = (p_sb[:, :SB_WIDTH] * (-LOG2E / math.sqrt(SB_HEAD_DIM))).astype(_BF16)
    ksb_ref[...] = p_sb[:, SB_WIDTH:2 * SB_WIDTH].astype(_BF16)
    vsb_ref[...] = p_sb[:, 2 * SB_WIDTH:].astype(_BF16)

    p_lat = _dot(u, w_lat_ref[...])
    c_q = _rms(p_lat[:, :Q_LORA], g_q_ref[...]).astype(_BF16)
    c_kv = _rms(p_lat[:, Q_LORA:Q_LORA + KV_LORA], g_kv_ref[...]).astype(_BF16)
    k_rope = p_lat[:, Q_LORA + KV_LORA:]

    tab = tab_ref[...]
    cos_q, slo_q, shi_q, cos_k, slo_k, shi_k = [tab[:, i * LANE:(i + 1) * LANE] for i in range(6)]
    k_rope = _rope(k_rope, cos_k, slo_k, shi_k)

    q = _dot(c_q, wq_ref[...])
    k_nope = _dot(c_kv, wk_ref[...])
    for hh in range(MLA_HEADS):
        sl = slice(hh * MLA_SLOT, (hh + 1) * MLA_SLOT)
        qm_ref[:, sl] = _rope(q[:, sl], cos_q, slo_q, shi_q).astype(_BF16)
        km_ref[:, sl] = (k_nope[:, sl] + k_rope).astype(_BF16)
    vm_ref[...] = _dot(c_kv, wv_ref[...]).astype(_BF16)


def _const_spec(shape):
    nd = len(shape)
    return pl.BlockSpec(shape, lambda *_: (0,) * nd, pipeline_mode=pl.Buffered(1))


def _in_proj(h2d, tab, g_pre, w_sb, w_lat, g_q, g_kv, wq, wk, wv, l_pad):
    m = h2d.shape[0]
    t = IN_ROWS
    tiles_per_seq = l_pad // t
    row = lambda w: pl.BlockSpec((t, w), lambda i: (i, 0))
    out_widths = (SB_WIDTH, SB_WIDTH, SB_WIDTH, MLA_HEADS * MLA_SLOT, MLA_HEADS * MLA_SLOT, MLA_WIDTH)
    return pl.pallas_call(
        _in_proj_kernel,
        grid=(m // t,),
        in_specs=[row(D_MODEL),
                  pl.BlockSpec((t, 6 * LANE), lambda i: (i % tiles_per_seq, 0)),
                  _const_spec(g_pre.shape), _const_spec(w_sb.shape), _const_spec(w_lat.shape),
                  _const_spec(g_q.shape), _const_spec(g_kv.shape),
                  _const_spec(wq.shape), _const_spec(wk.shape), _const_spec(wv.shape)],
        out_specs=[row(w) for w in out_widths],
        out_shape=[jax.ShapeDtypeStruct((m, w), _BF16) for w in out_widths],
        compiler_params=pltpu.CompilerParams(dimension_semantics=("arbitrary",),
                                             vmem_limit_bytes=52 * 1024 * 1024),
        name="in_proj",
    )(h2d, tab, g_pre, w_sb, w_lat, g_q, g_kv, wq, wk, wv)


def _sb_kernel(q_ref, k_ref, vt_ref, tri_ref, o_ref, qh_ref, c_ref, acc_ref):
    tq, tk = SEQ_TILE, SB_KEYS
    qi = pl.program_id(1)
    low_head = lax.broadcasted_iota(jnp.int32, (tq, LANE), 1) < SB_HEAD_DIM
    for hp in range(SB_WIDTH // LANE):
        qp = q_ref[0, :, hp * LANE:(hp + 1) * LANE]
        zero = jnp.zeros_like(qp)
        qh_ref[2 * hp] = jnp.where(low_head, qp, zero)
        qh_ref[2 * hp + 1] = jnp.where(low_head, zero, qp)
    tri = tri_ref[...]
    c_ref[...] = jnp.zeros_like(c_ref)
    acc_ref[...] = jnp.zeros_like(acc_ref)

    def block(j, masked, q_lo=0):
        start = pl.multiple_of(j * tk, tk)
        nq = tq - q_lo
        qs = slice(q_lo, tq)
        if masked:
            key = start + lax.broadcasted_iota(jnp.int32, (tk, nq), 0)
            qry = qi * tq + q_lo + lax.broadcasted_iota(jnp.int32, (tk, nq), 1)
            valid = key < qry

        def scores(h):
            hp = h // 2
            kb = k_ref[0, pl.ds(start, tk), hp * LANE:(hp + 1) * LANE]
            return lax.dot_general(kb, qh_ref[h, qs, :], _NT, preferred_element_type=_F32)

        def cumsum(zn):
            low = jnp.minimum(zn, 0.0)
            lm = low - jnp.log2(1.0 + jnp.exp2((low - zn) + low))
            ls = lm - zn
            if masked:
                lm = jnp.where(valid, lm, 0.0)
            lm = lm.astype(_BF16)
            cs = _dot(tri, lm)
            return ls, cs, cs[0:1, :] + lm[0:1, :].astype(_F32)

        def accumulate(h, ls, cs, total):
            a = jnp.exp2(ls + cs + c_ref[h:h + 1, qs])
            if masked:
                a = jnp.where(valid, a, 0.0)
            rows = slice(h * SB_HEAD_DIM, (h + 1) * SB_HEAD_DIM)
            acc_ref[rows, qs] += _dot(vt_ref[0, j, rows, :], a.astype(_BF16))
            c_ref[h:h + 1, qs] += total

        groups = [range(g, g + SB_GROUP) for g in range(0, SB_HEADS, SB_GROUP)]
        zn = {h: scores(h) for h in groups[0]}
        st = {}
        for gi, group in enumerate(groups):
            if gi + 1 < len(groups):
                zn.update({h: scores(h) for h in groups[gi + 1]})
            st.update({h: cumsum(zn[h]) for h in group})
            if gi >= 1:
                for h in groups[gi - 1]:
                    accumulate(h, *st[h])
        for h in groups[-1]:
            accumulate(h, *st[h])

    per_tile = tq // tk
    for d in range(per_tile - 1, -1, -1):
        block(qi * per_tile + d, True, q_lo=d * tk)

    n_off = qi * per_tile

    def more(carry):
        i, live = carry
        return jnp.logical_and(i < n_off, live > 0)

    def body(carry):
        i, _ = carry
        block(n_off - 1 - i, False)
        return i + 1, (jnp.max(c_ref[...]) > UNDERFLOW_LOG2).astype(jnp.int32)

    lax.while_loop(more, body, (jnp.int32(0), jnp.int32(1)))
    for hp in range(SB_WIDTH // LANE):
        o_ref[0, :, hp * LANE:(hp + 1) * LANE] = acc_ref[hp * LANE:(hp + 1) * LANE, :].T


def _sb_attention(q, k, vt, tri):
    b, l_pad, _ = q.shape
    tq = SEQ_TILE
    return pl.pallas_call(
        _sb_kernel,
        grid=(b, l_pad // tq),
        in_specs=[pl.BlockSpec((1, tq, SB_WIDTH), lambda bi, qi: (bi, qi, 0)),
                  pl.BlockSpec((1, l_pad, SB_WIDTH), lambda bi, qi: (bi, 0, 0)),
                  pl.BlockSpec((1,) + vt.shape[1:], lambda bi, qi: (bi, 0, 0, 0)),
                  pl.BlockSpec(tri.shape, lambda bi, qi: (0, 0))],
        out_specs=pl.BlockSpec((1, tq, SB_WIDTH), lambda bi, qi: (bi, qi, 0)),
        out_shape=jax.ShapeDtypeStruct((b, l_pad, SB_WIDTH), _F32),
        scratch_shapes=[pltpu.VMEM((SB_HEADS, tq, LANE), _BF16),
                        pltpu.VMEM((SB_HEADS, tq), _F32),
                        pltpu.VMEM((SB_WIDTH, tq), _F32)],
        compiler_params=pltpu.CompilerParams(dimension_semantics=("arbitrary",) * 2,
                                             vmem_limit_bytes=40 * 1024 * 1024),
        name="sb_attention",
    )(q, k, vt, tri)


def _mla_kernel(q_ref, k_ref, vt_ref, o_ref, s_ref, m_ref, acc_ref):
    tq = tk = SEQ_TILE
    qi = pl.program_id(1)
    m_ref[...] = jnp.full_like(m_ref, NEG_BIG)
    acc_ref[...] = jnp.zeros_like(acc_ref)
    lead = range(MLA_GROUP)
    rest = range(MLA_GROUP, MLA_HEADS)

    def scores(j, h):
        sl = slice(h * MLA_SLOT, (h + 1) * MLA_SLOT)
        start = pl.multiple_of(j * tk, tk)
        return lax.dot_general(k_ref[0, pl.ds(start, tk), sl], q_ref[0, :, sl], _NT,
                               preferred_element_type=_F32)

    def lead_scores(j):
        for h in lead:
            s_ref[h] = scores(j, h)

    def block(j, j_next, masked):
        if masked:
            valid = lax.broadcasted_iota(jnp.int32, (tk, tq), 0) <= lax.broadcasted_iota(jnp.int32, (tk, tq), 1)

        def softmax(h, s):
            if masked:
                s = jnp.where(valid, s, NEG_BIG)
            m_old = m_ref[h:h + 1, :]
            m_new = jnp.maximum(m_old, jnp.max(s, axis=0, keepdims=True))
            alpha = jnp.exp2(m_old - m_new)
            p = jnp.exp2(s - m_new)
            m_ref[h:h + 1, :] = m_new
            return alpha, p.astype(_BF16)

        def weighted_values(h, alpha, p):
            rows = slice(h * MLA_VROWS, (h + 1) * MLA_VROWS)
            acc_ref[rows, :] = alpha * acc_ref[rows, :] + _dot(vt_ref[0, j, rows, :], p)

        s_rest = [scores(j, h) for h in rest]
        probs = [softmax(h, s_ref[h]) for h in lead]
        for h, pr in zip(lead, probs):
            weighted_values(h, *pr)
        if j_next is not None:
            lead_scores(j_next)
        probs = [softmax(h, s) for h, s in zip(rest, s_rest)]
        for h, pr in zip(rest, probs):
            weighted_values(h, *pr)

    def two_blocks(i, carry):
        block(2 * i, 2 * i + 1, False)
        block(2 * i + 1, 2 * i + 2, False)
        return carry

    lead_scores(0)
    lax.fori_loop(0, qi // 2, two_blocks, 0)

    @pl.when(qi % 2 == 1)
    def _():
        block(qi - 1, qi, False)

    block(qi, None, True)

    for hp in range(MLA_WIDTH // LANE):
        pair = []
        for h in (2 * hp, 2 * hp + 1):
            base = h * MLA_VROWS
            pair.append(acc_ref[base:base + MLA_V, :] / acc_ref[base + MLA_V:base + MLA_V + 1, :])
        o_ref[0, :, hp * LANE:(hp + 1) * LANE] = jnp.concatenate(pair, axis=0).T


def _mla_attention(q, k, vt):
    b, l_pad, _ = q.shape
    tq = SEQ_TILE
    width = MLA_HEADS * MLA_SLOT
    return pl.pallas_call(
        _mla_kernel,
        grid=(b, l_pad // tq),
        in_specs=[pl.BlockSpec((1, tq, width), lambda bi, qi: (bi, qi, 0)),
                  pl.BlockSpec((1, l_pad, width), lambda bi, qi: (bi, 0, 0)),
                  pl.BlockSpec((1,) + vt.shape[1:], lambda bi, qi: (bi, 0, 0, 0))],
        out_specs=pl.BlockSpec((1, tq, MLA_WIDTH), lambda bi, qi: (bi, qi, 0)),
        out_shape=jax.ShapeDtypeStruct((b, l_pad, MLA_WIDTH), _F32),
        scratch_shapes=[pltpu.VMEM((MLA_GROUP, tq, tq), _F32),
                        pltpu.VMEM((MLA_HEADS, tq), _F32),
                        pltpu.VMEM((MLA_HEADS * MLA_VROWS, tq), _F32)],
        compiler_params=pltpu.CompilerParams(dimension_semantics=("arbitrary",) * 2,
                                             vmem_limit_bytes=48 * 1024 * 1024),
        name="mla_attention",
    )(q, k, vt)


def _post_kernel(h_ref, osb_ref, omla_ref, g_sb_ref, g_mla_ref, wo_ref, g_post_ref, g_ffn_ref,
                 wg_ref, wu_ref, wd_ref, g_out_ref, out_ref, f_ref, acc_ref):
    a = _rms(osb_ref[...], g_sb_ref[...]).astype(_BF16)
    b = _rms(omla_ref[...], g_mla_ref[...]).astype(_BF16)
    mix = _dot(a, wo_ref[:SB_WIDTH, :]) + _dot(b, wo_ref[SB_WIDTH:, :])
    h1 = h_ref[...] + _rms(mix, g_post_ref[...])
    out_ref[...] = h1
    f_ref[...] = _rms(h1, g_ffn_ref[...]).astype(_BF16)
    acc_ref[...] = jnp.zeros_like(acc_ref)

    def chunk(c, carry):
        f = f_ref[...]
        g = _dot(f, wg_ref[c])
        up = _dot(f, wu_ref[c])
        act = g * (1.0 / (1.0 + jnp.exp(-g))) * up
        acc_ref[...] += _dot(act.astype(_BF16), wd_ref[c])
        return carry

    lax.fori_loop(0, D_FF // FF_CHUNK, chunk, 0, unroll=True)
    out_ref[...] += _rms(acc_ref[...], g_out_ref[...])


def _post(h2d, o_sb, o_mla, g_sb, g_mla, wo, g_post, g_ffn, wg, wu, wd, g_out):
    m = h2d.shape[0]
    t = POST_ROWS
    row = lambda w: pl.BlockSpec((t, w), lambda i: (i, 0))
    consts = (g_sb, g_mla, wo, g_post, g_ffn, wg, wu, wd, g_out)
    return pl.pallas_call(
        _post_kernel,
        grid=(m // t,),
        in_specs=[row(D_MODEL), row(SB_WIDTH), row(MLA_WIDTH)] + [_const_spec(c.shape) for c in consts],
        out_specs=row(D_MODEL),
        out_shape=jax.ShapeDtypeStruct((m, D_MODEL), _F32),
        scratch_shapes=[pltpu.VMEM((t, D_MODEL), _BF16), pltpu.VMEM((t, D_MODEL), _F32)],
        compiler_params=pltpu.CompilerParams(dimension_semantics=("arbitrary",),
                                             vmem_limit_bytes=56 * 1024 * 1024),
        name="post_mix_ffn",
    )(h2d, o_sb, o_mla, *consts)


def _rope_table(l_pad):
    half = MLA_ROPE // 2
    inv_freq = 1.0 / (ROPE_THETA ** (jnp.arange(0, MLA_ROPE, 2, dtype=_F32) / MLA_ROPE))
    ang = jnp.arange(l_pad, dtype=_F32)[:, None] * inv_freq[None, :]
    cos, sin = jnp.cos(ang), jnp.sin(ang)
    z = lambda w: jnp.zeros((l_pad, w), _F32)
    tail = LANE - MLA_NOPE - MLA_ROPE
    cos_t = jnp.concatenate([jnp.ones((l_pad, MLA_NOPE), _F32), cos, cos, z(tail)], axis=1)
    sin_lo = jnp.concatenate([z(MLA_NOPE), -sin, z(half), z(tail)], axis=1)
    sin_hi = jnp.concatenate([z(MLA_NOPE), z(half), sin, z(tail)], axis=1)
    scale = LOG2E / math.sqrt(MLA_NOPE + MLA_ROPE)
    return jnp.concatenate([cos_t * scale, sin_lo * scale, sin_hi * scale, cos_t, sin_lo, sin_hi], axis=1)


def _triangular():
    s = jnp.arange(SB_KEYS)[:, None]
    j = jnp.arange(SB_KEYS)[None, :]
    return (j > s).astype(_BF16)


def _key_blocks_transposed(v, b, l_pad, tk):
    return v.reshape(b, l_pad // tk, tk, v.shape[-1]).transpose(0, 1, 3, 2)


def kernel(x, meta_tokens, w_in, q_lat_norm, kv_lat_norm, w_uq, w_ukv, sb_out_norm, mla_out_norm, w_o,
           pre_mix_norm, post_mix_norm, pre_ffn_norm, post_ffn_norm, w_gate, w_up, w_down):
    b, seq, _ = x.shape
    depth = w_in.shape[0]
    l_real = N_META + seq
    l_pad = -(-l_real // SEQ_TILE) * SEQ_TILE
    assert l_pad % IN_ROWS == 0 and (b * l_pad) % POST_ROWS == 0
    meta = jnp.broadcast_to(meta_tokens.astype(x.dtype)[None], (b, N_META, D_MODEL))
    h = jnp.concatenate([meta, x, jnp.zeros((b, l_pad - l_real, D_MODEL), x.dtype)], axis=1)
    h = h.reshape(b * l_pad, D_MODEL)
    tab = _rope_table(l_pad)
    tri = _triangular()
    o1 = 3 * SB_WIDTH
    o2 = o1 + Q_LORA
    o3 = o2 + KV_LORA
    n_chunks = D_FF // FF_CHUNK
    gain = lambda g: g.reshape(1, -1).astype(_F32)

    for layer in range(depth):
        wl = w_in[layer]
        w_sb = wl[:, :o1].astype(_BF16)
        zc = lambda w: jnp.zeros((D_MODEL, w), wl.dtype)
        w_lat = jnp.concatenate([wl[:, o1:o3], zc(MLA_NOPE), wl[:, o3:], zc(LANE - MLA_NOPE - MLA_ROPE)],
                                axis=1).astype(_BF16)
        wq = w_uq[layer].reshape(Q_LORA, MLA_HEADS, MLA_NOPE + MLA_ROPE)
        wq = jnp.pad(wq, ((0, 0), (0, 0), (0, MLA_SLOT - MLA_NOPE - MLA_ROPE)))
        wq = wq.reshape(Q_LORA, MLA_HEADS * MLA_SLOT).astype(_BF16)
        wkv = w_ukv[layer].reshape(KV_LORA, MLA_HEADS, MLA_NOPE + MLA_V)
        wk = jnp.pad(wkv[:, :, :MLA_NOPE], ((0, 0), (0, 0), (0, MLA_SLOT - MLA_NOPE)))
        wk = wk.reshape(KV_LORA, MLA_HEADS * MLA_SLOT).astype(_BF16)
        wv = wkv[:, :, MLA_NOPE:].reshape(KV_LORA, MLA_WIDTH).astype(_BF16)

        q_sb, k_sb, v_sb, q_m, k_m, v_m = _in_proj(
            h, tab, gain(pre_mix_norm[layer]), w_sb, w_lat, gain(q_lat_norm[layer]), gain(kv_lat_norm[layer]),
            wq, wk, wv, l_pad)
        seq3 = lambda a: a.reshape(b, l_pad, a.shape[-1])
        o_sb = _sb_attention(seq3(q_sb), seq3(k_sb), _key_blocks_transposed(v_sb, b, l_pad, SB_KEYS), tri)
        v_ones = jnp.concatenate([v_m.reshape(b * l_pad, MLA_HEADS, MLA_V),
                                  jnp.ones((b * l_pad, MLA_HEADS, MLA_VROWS - MLA_V), _BF16)], axis=2)
        v_ones = v_ones.reshape(b * l_pad, MLA_HEADS * MLA_VROWS)
        o_mla = _mla_attention(seq3(q_m), seq3(k_m), _key_blocks_transposed(v_ones, b, l_pad, SEQ_TILE))

        wg = w_gate[layer].reshape(D_MODEL, n_chunks, FF_CHUNK).transpose(1, 0, 2).astype(_BF16)
        wu = w_up[layer].reshape(D_MODEL, n_chunks, FF_CHUNK).transpose(1, 0, 2).astype(_BF16)
        wd = w_down[layer].reshape(n_chunks, FF_CHUNK, D_MODEL).astype(_BF16)
        h = _post(h, o_sb.reshape(b * l_pad, SB_WIDTH), o_mla.reshape(b * l_pad, MLA_WIDTH),
                  gain(sb_out_norm[layer]), gain(mla_out_norm[layer]), w_o[layer].astype(_BF16),
                  gain(post_mix_norm[layer]), gain(pre_ffn_norm[layer]), wg, wu, wd, gain(post_ffn_norm[layer]))

    return h.reshape(b, l_pad, D_MODEL)[:, N_META:l_real]
```

```python
import functools
import math

import jax
import jax.numpy as jnp
from jax import lax
from jax.experimental import pallas as pl
from jax.experimental.pallas import tpu as pltpu

D_MODEL = 1024
N_META = 16
SB_HEADS = 8
SB_HEAD_DIM = 64
SB_WIDTH = SB_HEADS * SB_HEAD_DIM
MLA_HEADS = 8
MLA_NOPE = 64
MLA_ROPE = 32
MLA_V = 64
Q_LORA = 384
KV_LORA = 256
MLA_WIDTH = MLA_HEADS * MLA_V
D_FF = 2816
ROPE_THETA = 10000.0
EPS = 1e-6

LANE = 128
MLA_SLOT = LANE
SEQ_TILE = 256
IN_ROWS = 544
POST_ROWS = 512
FF_CHUNK = 256
SB_KEYS = 128
SB_TILE = 256
MLA_VROWS = MLA_V + 16
SB_GROUP = 4
MLA_GROUP = 4
NEG_BIG = -1e30
LOG2E = math.log2(math.e)
UNDERFLOW_LOG2 = -150.0

_NT = (((1,), (1,)), ((), ()))
_F32 = jnp.float32
_BF16 = jnp.bfloat16


def _rms(x, gain):
    ms = jnp.mean(x * x, axis=-1, keepdims=True)
    return x * lax.rsqrt(ms + EPS) * gain


def _dot(a, b):
    return jnp.dot(a, b, preferred_element_type=_F32)


def _rope(x, cos, sin_lo, sin_hi):
    return x * cos + pltpu.roll(x, LANE - MLA_ROPE // 2, 1) * sin_lo + pltpu.roll(x, MLA_ROPE // 2, 1) * sin_hi


def _in_proj_kernel(h_ref, tab_ref, g_pre_ref, w_sb_ref, w_lat_ref, g_q_ref, g_kv_ref, wq_ref, wq_half_ref, wk_ref,
                    wv_ref, v_one_ref, qsb_ref, ksb_ref, vsb_ref, qm_ref, km_ref, vm_ref):
    u = _rms(h_ref[...], g_pre_ref[...]).astype(_BF16)
    p_sb = _dot(u, w_sb_ref[...])
    qsb_ref[...] = (p_sb[:, :SB_WIDTH] * (-LOG2E / math.sqrt(SB_HEAD_DIM))).astype(_BF16)
    ksb_ref[...] = p_sb[:, SB_WIDTH:2 * SB_WIDTH].astype(_BF16)
    vsb_ref[...] = p_sb[:, 2 * SB_WIDTH:].astype(_BF16)

    p_lat = _dot(u, w_lat_ref[...])
    c_q = _rms(p_lat[:, :Q_LORA], g_q_ref[...]).astype(_BF16)
    c_kv = _rms(p_lat[:, Q_LORA:Q_LORA + KV_LORA], g_kv_ref[...]).astype(_BF16)
    k_rope = p_lat[:, Q_LORA + KV_LORA:]

    tab = tab_ref[...]
    cos_q, sin_q, cos_k, slo_k, shi_k = [tab[:, i * LANE:(i + 1) * LANE] for i in range(5)]
    k_rope = _rope(k_rope, cos_k, slo_k, shi_k)

    q = _dot(c_q, wq_ref[...])
    q_half = _dot(c_q, wq_half_ref[...])
    k_nope = _dot(c_kv, wk_ref[...])
    for hh in range(MLA_HEADS):
        sl = slice(hh * MLA_SLOT, (hh + 1) * MLA_SLOT)
        qm_ref[:, sl] = (q[:, sl] * cos_q + q_half[:, sl] * sin_q).astype(_BF16)
        km_ref[:, sl] = (k_nope[:, sl] + k_rope).astype(_BF16)
    vm_ref[...] = (_dot(c_kv, wv_ref[...]) + v_one_ref[...]).astype(_BF16)


def _const_spec(shape):
    nd = len(shape)
    return pl.BlockSpec(shape, lambda *_: (0,) * nd, pipeline_mode=pl.Buffered(1))


def _in_proj(h2d, tab, g_pre, w_sb, w_lat, g_q, g_kv, wq, wq_half, wk, wv, v_one, l_pad):
    m = h2d.shape[0]
    t = IN_ROWS
    tiles_per_seq = l_pad // t
    row = lambda w: pl.BlockSpec((t, w), lambda i: (i, 0))
    out_widths = (SB_WIDTH, SB_WIDTH, SB_WIDTH, MLA_HEADS * MLA_SLOT, MLA_HEADS * MLA_SLOT, MLA_HEADS * MLA_VROWS)
    return pl.pallas_call(
        _in_proj_kernel,
        grid=(m // t,),
        in_specs=[row(D_MODEL),
                  pl.BlockSpec((t, tab.shape[1]), lambda i: (i % tiles_per_seq, 0)),
                  _const_spec(g_pre.shape), _const_spec(w_sb.shape), _const_spec(w_lat.shape),
                  _const_spec(g_q.shape), _const_spec(g_kv.shape),
                  _const_spec(wq.shape), _const_spec(wq_half.shape), _const_spec(wk.shape), _const_spec(wv.shape),
                  _const_spec(v_one.shape)],
        out_specs=[row(w) for w in out_widths],
        out_shape=[jax.ShapeDtypeStruct((m, w), _BF16) for w in out_widths],
        compiler_params=pltpu.CompilerParams(dimension_semantics=("arbitrary",),
                                             vmem_limit_bytes=52 * 1024 * 1024),
        name="in_proj",
    )(h2d, tab, g_pre, w_sb, w_lat, g_q, g_kv, wq, wq_half, wk, wv, v_one)


def _sb_kernel(q_ref, k_ref, vt_ref, tri_ref, o_ref, qh_ref, c_ref, acc_ref):
    tq, tk = SB_TILE, SB_KEYS
    qi = pl.program_id(1)
    low_head = lax.broadcasted_iota(jnp.int32, (tq, LANE), 1) < SB_HEAD_DIM
    for hp in range(SB_WIDTH // LANE):
        qp = q_ref[0, :, hp * LANE:(hp + 1) * LANE]
        zero = jnp.zeros_like(qp)
        qh_ref[2 * hp] = jnp.where(low_head, qp, zero)
        qh_ref[2 * hp + 1] = jnp.where(low_head, zero, qp)
    tri = tri_ref[...]
    c_ref[...] = jnp.zeros_like(c_ref)
    acc_ref[...] = jnp.zeros_like(acc_ref)

    def block(j, masked, q_lo=0):
        start = pl.multiple_of(j * tk, tk)
        nq = tq - q_lo
        qs = slice(q_lo, tq)
        if masked:
            key = start + lax.broadcasted_iota(jnp.int32, (tk, nq), 0)
            qry = qi * tq + q_lo + lax.broadcasted_iota(jnp.int32, (tk, nq), 1)
            valid = key < qry

        def scores(h):
            hp = h // 2
            kb = k_ref[0, pl.ds(start, tk), hp * LANE:(hp + 1) * LANE]
            return lax.dot_general(kb, qh_ref[h, qs, :], _NT, preferred_element_type=_F32)

        def cumsum(zn):
            low = jnp.minimum(zn, 0.0)
            lm = low - jnp.log2(1.0 + jnp.exp2((low - zn) + low))
            ls = lm - zn
            if masked:
                lm = jnp.where(valid, lm, 0.0)
            lm = lm.astype(_BF16)
            cs = _dot(tri, lm)
            return ls, cs, cs[0:1, :] + lm[0:1, :].astype(_F32)

        def accumulate(h, ls, cs, total):
            a = jnp.exp2(ls + cs + c_ref[h:h + 1, qs])
            if masked:
                a = jnp.where(valid, a, 0.0)
            rows = slice(h * SB_HEAD_DIM, (h + 1) * SB_HEAD_DIM)
            acc_ref[rows, qs] += _dot(vt_ref[0, j, rows, :], a.astype(_BF16))
            c_ref[h:h + 1, qs] += total

        groups = [range(g, g + SB_GROUP) for g in range(0, SB_HEADS, SB_GROUP)]
        zn = {h: scores(h) for h in groups[0]}
        st = {}
        for gi, group in enumerate(groups):
            if gi + 1 < len(groups):
                zn.update({h: scores(h) for h in groups[gi + 1]})
            st.update({h: cumsum(zn[h]) for h in group})
            if gi >= 1:
                for h in groups[gi - 1]:
                    accumulate(h, *st[h])
        for h in groups[-1]:
            accumulate(h, *st[h])

    per_tile = tq // tk
    for d in range(per_tile - 1, -1, -1):
        block(qi * per_tile + d, True, q_lo=d * tk)

    n_off = qi * per_tile

    def more(carry):
        i, live = carry
        return jnp.logical_and(i < n_off, live > 0)

    def body(carry):
        i, _ = carry
        block(n_off - 1 - i, False)
        return i + 1, (jnp.max(c_ref[...]) > UNDERFLOW_LOG2).astype(jnp.int32)

    lax.while_loop(more, body, (jnp.int32(0), jnp.int32(1)))
    for hp in range(SB_WIDTH // LANE):
        o_ref[0, :, hp * LANE:(hp + 1) * LANE] = acc_ref[hp * LANE:(hp + 1) * LANE, :].T


def _sb_attention(q, k, vt, tri):
    b, l_pad, _ = q.shape
    tq = SB_TILE
    return pl.pallas_call(
        _sb_kernel,
        grid=(b, l_pad // tq),
        in_specs=[pl.BlockSpec((1, tq, SB_WIDTH), lambda bi, qi: (bi, qi, 0)),
                  pl.BlockSpec((1, l_pad, SB_WIDTH), lambda bi, qi: (bi, 0, 0)),
                  pl.BlockSpec((1,) + vt.shape[1:], lambda bi, qi: (bi, 0, 0, 0)),
                  pl.BlockSpec(tri.shape, lambda bi, qi: (0, 0))],
        out_specs=pl.BlockSpec((1, tq, SB_WIDTH), lambda bi, qi: (bi, qi, 0)),
        out_shape=jax.ShapeDtypeStruct((b, l_pad, SB_WIDTH), _F32),
        scratch_shapes=[pltpu.VMEM((SB_HEADS, tq, LANE), _BF16),
                        pltpu.VMEM((SB_HEADS, tq), _F32),
                        pltpu.VMEM((SB_WIDTH, tq), _F32)],
        compiler_params=pltpu.CompilerParams(dimension_semantics=("arbitrary",) * 2,
                                             vmem_limit_bytes=40 * 1024 * 1024),
        name="sb_attention",
    )(q, k, vt, tri)


def _mla_kernel(q_ref, k_ref, vt_ref, o_ref, s_ref, m_ref, acc_ref):
    tq = tk = SEQ_TILE
    qi = pl.program_id(1)
    m_ref[...] = jnp.full_like(m_ref, NEG_BIG)
    acc_ref[...] = jnp.zeros_like(acc_ref)
    lead = range(MLA_GROUP)
    rest = range(MLA_GROUP, MLA_HEADS)

    def scores(j, h):
        sl = slice(h * MLA_SLOT, (h + 1) * MLA_SLOT)
        start = pl.multiple_of(j * tk, tk)
        return lax.dot_general(k_ref[0, pl.ds(start, tk), sl], q_ref[0, :, sl], _NT,
                               preferred_element_type=_F32)

    def lead_scores(j):
        for h in lead:
            s_ref[h] = scores(j, h)

    def block(j, j_next, masked):
        if masked:
            valid = lax.broadcasted_iota(jnp.int32, (tk, tq), 0) <= lax.broadcasted_iota(jnp.int32, (tk, tq), 1)

        def softmax(h, s):
            if masked:
                s = jnp.where(valid, s, NEG_BIG)
            m_old = m_ref[h:h + 1, :]
            m_new = jnp.maximum(m_old, jnp.max(s, axis=0, keepdims=True))
            alpha = jnp.exp2(m_old - m_new)
            p = jnp.exp2(s - m_new)
            m_ref[h:h + 1, :] = m_new
            return alpha, p.astype(_BF16)

        def weighted_values(h, alpha, p):
            rows = slice(h * MLA_VROWS, (h + 1) * MLA_VROWS)
            acc_ref[rows, :] = alpha * acc_ref[rows, :] + _dot(vt_ref[0, j, rows, :], p)

        s_rest = [scores(j, h) for h in rest]
        probs = [softmax(h, s_ref[h]) for h in lead]
        for h, pr in zip(lead, probs):
            weighted_values(h, *pr)
        if j_next is not None:
            lead_scores(j_next)
        probs = [softmax(h, s) for h, s in zip(rest, s_rest)]
        for h, pr in zip(rest, probs):
            weighted_values(h, *pr)

    def two_blocks(i, carry):
        block(2 * i, 2 * i + 1, False)
        block(2 * i + 1, 2 * i + 2, False)
        return carry

    lead_scores(0)
    lax.fori_loop(0, qi // 2, two_blocks, 0)

    @pl.when(qi % 2 == 1)
    def _():
        block(qi - 1, qi, False)

    block(qi, None, True)

    for hp in range(MLA_WIDTH // LANE):
        pair = []
        for h in (2 * hp, 2 * hp + 1):
            base = h * MLA_VROWS
            pair.append(acc_ref[base:base + MLA_V, :] / acc_ref[base + MLA_V:base + MLA_V + 1, :])
        o_ref[0, :, hp * LANE:(hp + 1) * LANE] = jnp.concatenate(pair, axis=0).T


def _mla_attention(q, k, vt):
    b, l_pad, _ = q.shape
    tq = SEQ_TILE
    width = MLA_HEADS * MLA_SLOT
    return pl.pallas_call(
        _mla_kernel,
        grid=(b, l_pad // tq),
        in_specs=[pl.BlockSpec((1, tq, width), lambda bi, qi: (bi, qi, 0)),
                  pl.BlockSpec((1, l_pad, width), lambda bi, qi: (bi, 0, 0)),
                  pl.BlockSpec((1,) + vt.shape[1:], lambda bi, qi: (bi, 0, 0, 0))],
        out_specs=pl.BlockSpec((1, tq, MLA_WIDTH), lambda bi, qi: (bi, qi, 0)),
        out_shape=jax.ShapeDtypeStruct((b, l_pad, MLA_WIDTH), _F32),
        scratch_shapes=[pltpu.VMEM((MLA_GROUP, tq, tq), _F32),
                        pltpu.VMEM((MLA_HEADS, tq), _F32),
                        pltpu.VMEM((MLA_HEADS * MLA_VROWS, tq), _F32)],
        compiler_params=pltpu.CompilerParams(dimension_semantics=("arbitrary",) * 2,
                                             vmem_limit_bytes=48 * 1024 * 1024),
        name="mla_attention",
    )(q, k, vt)


def _post_kernel(h_ref, osb_ref, omla_ref, g_sb_ref, g_mla_ref, wo_ref, g_post_ref, g_ffn_ref,
                 wg_ref, wu_ref, wd_ref, g_out_ref, out_ref, f_ref, acc_ref):
    a = _rms(osb_ref[...], g_sb_ref[...]).astype(_BF16)
    b = _rms(omla_ref[...], g_mla_ref[...]).astype(_BF16)
    mix = _dot(a, wo_ref[:SB_WIDTH, :]) + _dot(b, wo_ref[SB_WIDTH:, :])
    h1 = h_ref[...] + _rms(mix, g_post_ref[...])
    out_ref[...] = h1
    f_ref[...] = _rms(h1, g_ffn_ref[...]).astype(_BF16)
    acc_ref[...] = jnp.zeros_like(acc_ref)

    def chunk(c, carry):
        f = f_ref[...]
        g = _dot(f, wg_ref[c])
        up = _dot(f, wu_ref[c])
        act = g * (1.0 / (1.0 + jnp.exp(-g))) * up
        acc_ref[...] += _dot(act.astype(_BF16), wd_ref[c])
        return carry

    lax.fori_loop(0, D_FF // FF_CHUNK, chunk, 0, unroll=True)
    out_ref[...] += _rms(acc_ref[...], g_out_ref[...])


def _post(h2d, o_sb, o_mla, g_sb, g_mla, wo, g_post, g_ffn, wg, wu, wd, g_out):
    m = h2d.shape[0]
    t = POST_ROWS
    row = lambda w: pl.BlockSpec((t, w), lambda i: (i, 0))
    consts = (g_sb, g_mla, wo, g_post, g_ffn, wg, wu, wd, g_out)
    return pl.pallas_call(
        _post_kernel,
        grid=(m // t,),
        in_specs=[row(D_MODEL), row(SB_WIDTH), row(MLA_WIDTH)] + [_const_spec(c.shape) for c in consts],
        out_specs=row(D_MODEL),
        out_shape=jax.ShapeDtypeStruct((m, D_MODEL), _F32),
        scratch_shapes=[pltpu.VMEM((t, D_MODEL), _BF16), pltpu.VMEM((t, D_MODEL), _F32)],
        compiler_params=pltpu.CompilerParams(dimension_semantics=("arbitrary",),
                                             vmem_limit_bytes=56 * 1024 * 1024),
        name="post_mix_ffn",
    )(h2d, o_sb, o_mla, *consts)


def _rope_table(l_pad):
    half = MLA_ROPE // 2
    inv_freq = 1.0 / (ROPE_THETA ** (jnp.arange(0, MLA_ROPE, 2, dtype=_F32) / MLA_ROPE))
    ang = jnp.arange(l_pad, dtype=_F32)[:, None] * inv_freq[None, :]
    cos, sin = jnp.cos(ang), jnp.sin(ang)
    z = lambda w: jnp.zeros((l_pad, w), _F32)
    tail = LANE - MLA_NOPE - MLA_ROPE
    cos_t = jnp.concatenate([jnp.ones((l_pad, MLA_NOPE), _F32), cos, cos, z(tail)], axis=1)
    sin_lo = jnp.concatenate([z(MLA_NOPE), -sin, z(half), z(tail)], axis=1)
    sin_hi = jnp.concatenate([z(MLA_NOPE), z(half), sin, z(tail)], axis=1)
    scale = LOG2E / math.sqrt(MLA_NOPE + MLA_ROPE)
    return jnp.concatenate([cos_t * scale, (sin_hi - sin_lo) * scale, cos_t, sin_lo, sin_hi], axis=1)


def _triangular():
    s = jnp.arange(SB_KEYS)[:, None]
    j = jnp.arange(SB_KEYS)[None, :]
    return (j > s).astype(_BF16)


def _key_blocks_transposed(v, b, l_pad, tk):
    return v.reshape(b, l_pad // tk, tk, v.shape[-1]).transpose(0, 1, 3, 2)


def kernel(x, meta_tokens, w_in, q_lat_norm, kv_lat_norm, w_uq, w_ukv, sb_out_norm, mla_out_norm, w_o,
           pre_mix_norm, post_mix_norm, pre_ffn_norm, post_ffn_norm, w_gate, w_up, w_down):
    b, seq, _ = x.shape
    depth = w_in.shape[0]
    l_real = N_META + seq
    l_pad = -(-l_real // SEQ_TILE) * SEQ_TILE
    assert l_pad % IN_ROWS == 0 and (b * l_pad) % POST_ROWS == 0
    meta = jnp.broadcast_to(meta_tokens.astype(x.dtype)[None], (b, N_META, D_MODEL))
    h = jnp.concatenate([meta, x, jnp.zeros((b, l_pad - l_real, D_MODEL), x.dtype)], axis=1)
    h = h.reshape(b * l_pad, D_MODEL)
    tab = _rope_table(l_pad)
    tri = _triangular()
    v_one = (jnp.arange(MLA_HEADS * MLA_VROWS) % MLA_VROWS >= MLA_V).astype(_F32).reshape(1, -1)
    o1 = 3 * SB_WIDTH
    o2 = o1 + Q_LORA
    o3 = o2 + KV_LORA
    n_chunks = D_FF // FF_CHUNK
    gain = lambda g: g.reshape(1, -1).astype(_F32)

    for layer in range(depth):
        wl = w_in[layer]
        w_sb = wl[:, :o1].astype(_BF16)
        zc = lambda w: jnp.zeros((D_MODEL, w), wl.dtype)
        w_lat = jnp.concatenate([wl[:, o1:o3], zc(MLA_NOPE), wl[:, o3:], zc(LANE - MLA_NOPE - MLA_ROPE)],
                                axis=1).astype(_BF16)
        wq = w_uq[layer].reshape(Q_LORA, MLA_HEADS, MLA_NOPE + MLA_ROPE)
        half = MLA_ROPE // 2
        wq_half = jnp.concatenate([jnp.zeros_like(wq[:, :, :MLA_NOPE]), -wq[:, :, MLA_NOPE + half:],
                                   wq[:, :, MLA_NOPE:MLA_NOPE + half]], axis=2)
        slot_pad = ((0, 0), (0, 0), (0, MLA_SLOT - MLA_NOPE - MLA_ROPE))
        wq_half = jnp.pad(wq_half, slot_pad).reshape(Q_LORA, MLA_HEADS * MLA_SLOT).astype(_BF16)
        wq = jnp.pad(wq, slot_pad).reshape(Q_LORA, MLA_HEADS * MLA_SLOT).astype(_BF16)
        wkv = w_ukv[layer].reshape(KV_LORA, MLA_HEADS, MLA_NOPE + MLA_V)
        wk = jnp.pad(wkv[:, :, :MLA_NOPE], ((0, 0), (0, 0), (0, MLA_SLOT - MLA_NOPE)))
        wk = wk.reshape(KV_LORA, MLA_HEADS * MLA_SLOT).astype(_BF16)
        wv = jnp.pad(wkv[:, :, MLA_NOPE:], ((0, 0), (0, 0), (0, MLA_VROWS - MLA_V)))
        wv = wv.reshape(KV_LORA, MLA_HEADS * MLA_VROWS).astype(_BF16)

        q_sb, k_sb, v_sb, q_m, k_m, v_m = _in_proj(
            h, tab, gain(pre_mix_norm[layer]), w_sb, w_lat, gain(q_lat_norm[layer]), gain(kv_lat_norm[layer]),
            wq, wq_half, wk, wv, v_one, l_pad)
        seq3 = lambda a: a.reshape(b, l_pad, a.shape[-1])
        o_sb = _sb_attention(seq3(q_sb), seq3(k_sb), _key_blocks_transposed(v_sb, b, l_pad, SB_KEYS), tri)
        o_mla = _mla_attention(seq3(q_m), seq3(k_m), _key_blocks_transposed(v_m, b, l_pad, SEQ_TILE))

        wg = w_gate[layer].reshape(D_MODEL, n_chunks, FF_CHUNK).transpose(1, 0, 2).astype(_BF16)
        wu = w_up[layer].reshape(D_MODEL, n_chunks, FF_CHUNK).transpose(1, 0, 2).astype(_BF16)
        wd = w_down[layer].reshape(n_chunks, FF_CHUNK, D_MODEL).astype(_BF16)
        h = _post(h, o_sb.reshape(b * l_pad, SB_WIDTH), o_mla.reshape(b * l_pad, MLA_WIDTH),
                  gain(sb_out_norm[layer]), gain(mla_out_norm[layer]), w_o[layer].astype(_BF16),
                  gain(post_mix_norm[layer]), gain(pre_ffn_norm[layer]), wg, wu, wd, gain(post_ffn_norm[layer]))

    return h.reshape(b, l_pad, D_MODEL)[:, N_META:l_real]
```

```python
import functools
import math

import jax
import jax.numpy as jnp
from jax import lax
from jax.experimental import pallas as pl
from jax.experimental.pallas import tpu as pltpu

D_MODEL = 1024
N_META = 16
SB_HEADS = 8
SB_HEAD_DIM = 64
SB_WIDTH = SB_HEADS * SB_HEAD_DIM
MLA_HEADS = 8
MLA_NOPE = 64
MLA_ROPE = 32
MLA_V = 64
Q_LORA = 384
KV_LORA = 256
MLA_WIDTH = MLA_HEADS * MLA_V
D_FF = 2816
ROPE_THETA = 10000.0
EPS = 1e-6

LANE = 128
MLA_SLOT = LANE
SEQ_TILE = 256
IN_ROWS = 544
POST_ROWS = 512
FF_CHUNK = 256
SB_KEYS = 128
SB_TILE = 256
MLA_VROWS = MLA_V + 16
SB_GROUP = 4
MLA_GROUP = 4
NEG_BIG = -1e30
LOG2E = math.log2(math.e)
UNDERFLOW_LOG2 = -150.0

_NT = (((1,), (1,)), ((), ()))
_F32 = jnp.float32
_BF16 = jnp.bfloat16


def _rms(x, gain):
    ms = jnp.mean(x * x, axis=-1, keepdims=True)
    return x * lax.rsqrt(ms + EPS) * gain


def _dot(a, b):
    return jnp.dot(a, b, preferred_element_type=_F32)


def _rope(x, cos, sin_lo, sin_hi):
    return x * cos + pltpu.roll(x, LANE - MLA_ROPE // 2, 1) * sin_lo + pltpu.roll(x, MLA_ROPE // 2, 1) * sin_hi


def _in_proj_kernel(h_ref, tab_ref, g_pre_ref, w_sb_ref, w_lat_ref, g_q_ref, g_kv_ref, wq_ref, wq_half_ref, wk_ref,
                    wv_ref, v_one_ref, qsb_ref, ksb_ref, vsb_ref, qm_ref, km_ref, vm_ref):
    u = _rms(h_ref[...], g_pre_ref[...]).astype(_BF16)
    p_sb = _dot(u, w_sb_ref[...])
    qsb_ref[...] = (p_sb[:, :SB_WIDTH] * (-LOG2E / math.sqrt(SB_HEAD_DIM))).astype(_BF16)
    ksb_ref[...] = p_sb[:, SB_WIDTH:2 * SB_WIDTH].astype(_BF16)
    vsb_ref[...] = p_sb[:, 2 * SB_WIDTH:].astype(_BF16)

    p_lat = _dot(u, w_lat_ref[...])
    c_q = _rms(p_lat[:, :Q_LORA], g_q_ref[...]).astype(_BF16)
    c_kv = _rms(p_lat[:, Q_LORA:Q_LORA + KV_LORA], g_kv_ref[...]).astype(_BF16)
    k_rope = p_lat[:, Q_LORA + KV_LORA:]

    tab = tab_ref[...]
    cos_q, sin_q, cos_k, slo_k, shi_k = [tab[:, i * LANE:(i + 1) * LANE] for i in range(5)]
    k_rope = _rope(k_rope, cos_k, slo_k, shi_k)

    q = _dot(c_q, wq_ref[...])
    q_half = _dot(c_q, wq_half_ref[...])
    k_nope = _dot(c_kv, wk_ref[...])
    for hh in range(MLA_HEADS):
        sl = slice(hh * MLA_SLOT, (hh + 1) * MLA_SLOT)
        qm_ref[:, sl] = (q[:, sl] * cos_q + q_half[:, sl] * sin_q).astype(_BF16)
        km_ref[:, sl] = (k_nope[:, sl] + k_rope).astype(_BF16)
    vm_ref[...] = (_dot(c_kv, wv_ref[...]) + v_one_ref[...]).astype(_BF16)


def _const_spec(shape):
    nd = len(shape)
    return pl.BlockSpec(shape, lambda *_: (0,) * nd, pipeline_mode=pl.Buffered(1))


def _in_proj(h2d, tab, g_pre, w_sb, w_lat, g_q, g_kv, wq, wq_half, wk, wv, v_one, l_pad):
    m = h2d.shape[0]
    t = IN_ROWS
    tiles_per_seq = l_pad // t
    row = lambda w: pl.BlockSpec((t, w), lambda i: (i, 0))
    out_widths = (SB_WIDTH, SB_WIDTH, SB_WIDTH, MLA_HEADS * MLA_SLOT, MLA_HEADS * MLA_SLOT, MLA_HEADS * MLA_VROWS)
    return pl.pallas_call(
        _in_proj_kernel,
        grid=(m // t,),
        in_specs=[row(D_MODEL),
                  pl.BlockSpec((t, tab.shape[1]), lambda i: (i % tiles_per_seq, 0)),
                  _const_spec(g_pre.shape), _const_spec(w_sb.shape), _const_spec(w_lat.shape),
                  _const_spec(g_q.shape), _const_spec(g_kv.shape),
                  _const_spec(wq.shape), _const_spec(wq_half.shape), _const_spec(wk.shape), _const_spec(wv.shape),
                  _const_spec(v_one.shape)],
        out_specs=[row(w) for w in out_widths],
        out_shape=[jax.ShapeDtypeStruct((m, w), _BF16) for w in out_widths],
        compiler_params=pltpu.CompilerParams(dimension_semantics=("arbitrary",),
                                             vmem_limit_bytes=52 * 1024 * 1024),
        name="in_proj",
    )(h2d, tab, g_pre, w_sb, w_lat, g_q, g_kv, wq, wq_half, wk, wv, v_one)


def _sb_kernel(q_ref, k_ref, vt_ref, tri_ref, o_ref, qh_ref, c_ref, acc_ref):
    tq, tk = SB_TILE, SB_KEYS
    qi = pl.program_id(1)
    low_head = lax.broadcasted_iota(jnp.int32, (tq, LANE), 1) < SB_HEAD_DIM
    for hp in range(SB_WIDTH // LANE):
        qp = q_ref[0, :, hp * LANE:(hp + 1) * LANE]
        zero = jnp.zeros_like(qp)
        qh_ref[2 * hp] = jnp.where(low_head, qp, zero)
        qh_ref[2 * hp + 1] = jnp.where(low_head, zero, qp)
    tri = tri_ref[...]
    c_ref[...] = jnp.zeros_like(c_ref)
    acc_ref[...] = jnp.zeros_like(acc_ref)

    def block(j, masked, q_lo=0):
        start = pl.multiple_of(j * tk, tk)
        nq = tq - q_lo
        qs = slice(q_lo, tq)
        if masked:
            key = start + lax.broadcasted_iota(jnp.int32, (tk, nq), 0)
            qry = qi * tq + q_lo + lax.broadcasted_iota(jnp.int32, (tk, nq), 1)
            valid = key < qry

        def scores(h):
            hp = h // 2
            kb = k_ref[0, pl.ds(start, tk), hp * LANE:(hp + 1) * LANE]
            return lax.dot_general(kb, qh_ref[h, qs, :], _NT, preferred_element_type=_F32)

        def cumsum(zn):
            low = jnp.minimum(zn, 0.0)
            lm = low - jnp.log2(1.0 + jnp.exp2((low - zn) + low))
            ls = lm - zn
            if masked:
                lm = jnp.where(valid, lm, 0.0)
            lm = lm.astype(_BF16)
            cs = _dot(tri, lm)
            return ls, cs, cs[0:1, :] + lm[0:1, :].astype(_F32)

        def accumulate(h, ls, cs, total):
            a = jnp.exp2(ls + cs + c_ref[h:h + 1, qs])
            if masked:
                a = jnp.where(valid, a, 0.0)
            rows = slice(h * SB_HEAD_DIM, (h + 1) * SB_HEAD_DIM)
            acc_ref[rows, qs] += _dot(vt_ref[0, j, rows, :], a.astype(_BF16))
            c_ref[h:h + 1, qs] += total

        groups = [range(g, g + SB_GROUP) for g in range(0, SB_HEADS, SB_GROUP)]
        zn = {h: scores(h) for h in groups[0]}
        st = {}
        for gi, group in enumerate(groups):
            if gi + 1 < len(groups):
                zn.update({h: scores(h) for h in groups[gi + 1]})
            st.update({h: cumsum(zn[h]) for h in group})
            if gi >= 1:
                for h in groups[gi - 1]:
                    accumulate(h, *st[h])
        for h in groups[-1]:
            accumulate(h, *st[h])

    per_tile = tq // tk
    for d in range(per_tile - 1, -1, -1):
        block(qi * per_tile + d, True, q_lo=d * tk)

    n_off = qi * per_tile

    def more(carry):
        i, live = carry
        return jnp.logical_and(i < n_off, live > 0)

    def body(carry):
        i, _ = carry
        block(n_off - 1 - i, False)
        return i + 1, (jnp.max(c_ref[...]) > UNDERFLOW_LOG2).astype(jnp.int32)

    lax.while_loop(more, body, (jnp.int32(0), jnp.int32(1)))
    for hp in range(SB_WIDTH // LANE):
        o_ref[0, :, hp * LANE:(hp + 1) * LANE] = acc_ref[hp * LANE:(hp + 1) * LANE, :].T


def _sb_attention(q, k, vt, tri):
    b, l_pad, _ = q.shape
    tq = SB_TILE
    return pl.pallas_call(
        _sb_kernel,
        grid=(b, l_pad // tq),
        in_specs=[pl.BlockSpec((1, tq, SB_WIDTH), lambda bi, qi: (bi, qi, 0)),
                  pl.BlockSpec((1, l_pad, SB_WIDTH), lambda bi, qi: (bi, 0, 0)),
                  pl.BlockSpec((1,) + vt.shape[1:], lambda bi, qi: (bi, 0, 0, 0)),
                  pl.BlockSpec(tri.shape, lambda bi, qi: (0, 0))],
        out_specs=pl.BlockSpec((1, tq, SB_WIDTH), lambda bi, qi: (bi, qi, 0)),
        out_shape=jax.ShapeDtypeStruct((b, l_pad, SB_WIDTH), _F32),
        scratch_shapes=[pltpu.VMEM((SB_HEADS, tq, LANE), _BF16),
                        pltpu.VMEM((SB_HEADS, tq), _F32),
                        pltpu.VMEM((SB_WIDTH, tq), _F32)],
        compiler_params=pltpu.CompilerParams(dimension_semantics=("arbitrary",) * 2,
                                             vmem_limit_bytes=40 * 1024 * 1024),
        name="sb_attention",
    )(q, k, vt, tri)


def _mla_kernel(q_ref, k_ref, vt_ref, o_ref, s_ref, m_ref, acc_ref):
    tq = tk = SEQ_TILE
    qi = pl.program_id(1)
    m_ref[...] = jnp.full_like(m_ref, NEG_BIG)
    acc_ref[...] = jnp.zeros_like(acc_ref)
    lead = range(MLA_GROUP)
    rest = range(MLA_GROUP, MLA_HEADS)

    def scores(j, h):
        sl = slice(h * MLA_SLOT, (h + 1) * MLA_SLOT)
        start = pl.multiple_of(j * tk, tk)
        return lax.dot_general(k_ref[0, pl.ds(start, tk), sl], q_ref[0, :, sl], _NT,
                               preferred_element_type=_F32)

    def lead_scores(j):
        for h in lead:
            s_ref[h] = scores(j, h)

    def block(j, j_next, masked):
        if masked:
            valid = lax.broadcasted_iota(jnp.int32, (tk, tq), 0) <= lax.broadcasted_iota(jnp.int32, (tk, tq), 1)

        def softmax(h, s):
            if masked:
                s = jnp.where(valid, s, NEG_BIG)
            m_old = m_ref[h:h + 1, :]
            m_new = jnp.maximum(m_old, jnp.max(s, axis=0, keepdims=True))
            alpha = jnp.exp2(m_old - m_new)
            p = jnp.exp2(s - m_new)
            m_ref[h:h + 1, :] = m_new
            return alpha, p.astype(_BF16)

        def weighted_values(h, alpha, p):
            rows = slice(h * MLA_VROWS, (h + 1) * MLA_VROWS)
            acc_ref[rows, :] = alpha * acc_ref[rows, :] + _dot(vt_ref[0, j, rows, :], p)

        s_rest = [scores(j, h) for h in rest]
        probs = [softmax(h, s_ref[h]) for h in lead]
        for h, pr in zip(lead, probs):
            weighted_values(h, *pr)
        if j_next is not None:
            lead_scores(j_next)
        probs = [softmax(h, s) for h, s in zip(rest, s_rest)]
        for h, pr in zip(rest, probs):
            weighted_values(h, *pr)

    def two_blocks(i, carry):
        block(2 * i, 2 * i + 1, False)
        block(2 * i + 1, 2 * i + 2, False)
        return carry

    lead_scores(0)
    lax.fori_loop(0, qi // 2, two_blocks, 0)

    @pl.when(qi % 2 == 1)
    def _():
        block(qi - 1, qi, False)

    block(qi, None, True)

    for hp in range(MLA_WIDTH // LANE):
        pair = []
        for h in (2 * hp, 2 * hp + 1):
            base = h * MLA_VROWS
            pair.append(acc_ref[base:base + MLA_V, :] / acc_ref[base + MLA_V:base + MLA_V + 1, :])
        o_ref[0, :, hp * LANE:(hp + 1) * LANE] = jnp.concatenate(pair, axis=0).T


def _mla_attention(q, k, vt):
    b, l_pad, _ = q.shape
    tq = SEQ_TILE
    width = MLA_HEADS * MLA_SLOT
    return pl.pallas_call(
        _mla_kernel,
        grid=(b, l_pad // tq),
        in_specs=[pl.BlockSpec((1, tq, width), lambda bi, qi: (bi, qi, 0)),
                  pl.BlockSpec((1, l_pad, width), lambda bi, qi: (bi, 0, 0)),
                  pl.BlockSpec((1,) + vt.shape[1:], lambda bi, qi: (bi, 0, 0, 0))],
        out_specs=pl.BlockSpec((1, tq, MLA_WIDTH), lambda bi, qi: (bi, qi, 0)),
        out_shape=jax.ShapeDtypeStruct((b, l_pad, MLA_WIDTH), _F32),
        scratch_shapes=[pltpu.VMEM((MLA_GROUP, tq, tq), _F32),
                        pltpu.VMEM((MLA_HEADS, tq), _F32),
                        pltpu.VMEM((MLA_HEADS * MLA_VROWS, tq), _F32)],
        compiler_params=pltpu.CompilerParams(dimension_semantics=("arbitrary",) * 2,
                                             vmem_limit_bytes=48 * 1024 * 1024),
        name="mla_attention",
    )(q, k, vt)


def _post_kernel(h_ref, osb_ref, omla_ref, g_sb_ref, g_mla_ref, wo_ref, g_post_ref, g_ffn_ref,
                 wg_ref, wu_ref, wd_ref, g_out_ref, out_ref, f_ref, acc_ref):
    a = _rms(osb_ref[...], g_sb_ref[...]).astype(_BF16)
    b = _rms(omla_ref[...], g_mla_ref[...]).astype(_BF16)
    mix = _dot(a, wo_ref[:SB_WIDTH, :]) + _dot(b, wo_ref[SB_WIDTH:, :])
    h1 = h_ref[...] + _rms(mix, g_post_ref[...])
    out_ref[...] = h1
    f_ref[...] = _rms(h1, g_ffn_ref[...]).astype(_BF16)
    acc_ref[...] = jnp.zeros_like(acc_ref)

    def chunk(c, carry):
        f = f_ref[...]
        g = _dot(f, wg_ref[c])
        up = _dot(f, wu_ref[c])
        act = g * (1.0 / (1.0 + jnp.exp(-g))) * up
        acc_ref[...] += _dot(act.astype(_BF16), wd_ref[c])
        return carry

    lax.fori_loop(0, D_FF // FF_CHUNK, chunk, 0, unroll=True)
    out_ref[...] += _rms(acc_ref[...], g_out_ref[...])


def _post(h2d, o_sb, o_mla, g_sb, g_mla, wo, g_post, g_ffn, wg, wu, wd, g_out, real_only=None):
    m = h2d.shape[0]
    t = POST_ROWS
    if real_only is None:
        grid = (m // t,)
        row = lambda w: pl.BlockSpec((t, w), lambda i: (i, 0))
        out_spec, out_rows = row(D_MODEL), m
    else:
        b, l_pad, first, n_rows = real_only
        assert n_rows % t == 0
        tiles = n_rows // t
        grid = (b, tiles)
        row = lambda w: pl.BlockSpec((pl.Element(t), pl.Element(w)),
                                     lambda bi, ti: (pl.multiple_of(bi * l_pad + first + ti * t, 8), 0))
        out_spec, out_rows = pl.BlockSpec((t, D_MODEL), lambda bi, ti: (bi * tiles + ti, 0)), b * n_rows
    consts = (g_sb, g_mla, wo, g_post, g_ffn, wg, wu, wd, g_out)
    return pl.pallas_call(
        _post_kernel,
        grid=grid,
        in_specs=[row(D_MODEL), row(SB_WIDTH), row(MLA_WIDTH)] + [_const_spec(c.shape) for c in consts],
        out_specs=out_spec,
        out_shape=jax.ShapeDtypeStruct((out_rows, D_MODEL), _F32),
        scratch_shapes=[pltpu.VMEM((t, D_MODEL), _BF16), pltpu.VMEM((t, D_MODEL), _F32)],
        compiler_params=pltpu.CompilerParams(dimension_semantics=("arbitrary",) * len(grid),
                                             vmem_limit_bytes=56 * 1024 * 1024),
        name="post_mix_ffn",
    )(h2d, o_sb, o_mla, *consts)


def _rope_table(l_pad):
    half = MLA_ROPE // 2
    inv_freq = 1.0 / (ROPE_THETA ** (jnp.arange(0, MLA_ROPE, 2, dtype=_F32) / MLA_ROPE))
    ang = jnp.arange(l_pad, dtype=_F32)[:, None] * inv_freq[None, :]
    cos, sin = jnp.cos(ang), jnp.sin(ang)
    z = lambda w: jnp.zeros((l_pad, w), _F32)
    tail = LANE - MLA_NOPE - MLA_ROPE
    cos_t = jnp.concatenate([jnp.ones((l_pad, MLA_NOPE), _F32), cos, cos, z(tail)], axis=1)
    sin_lo = jnp.concatenate([z(MLA_NOPE), -sin, z(half), z(tail)], axis=1)
    sin_hi = jnp.concatenate([z(MLA_NOPE), z(half), sin, z(tail)], axis=1)
    scale = LOG2E / math.sqrt(MLA_NOPE + MLA_ROPE)
    return jnp.concatenate([cos_t * scale, (sin_hi - sin_lo) * scale, cos_t, sin_lo, sin_hi], axis=1)


def _triangular():
    s = jnp.arange(SB_KEYS)[:, None]
    j = jnp.arange(SB_KEYS)[None, :]
    return (j > s).astype(_BF16)


def _key_blocks_transposed(v, b, l_pad, tk):
    return v.reshape(b, l_pad // tk, tk, v.shape[-1]).transpose(0, 1, 3, 2)


def kernel(x, meta_tokens, w_in, q_lat_norm, kv_lat_norm, w_uq, w_ukv, sb_out_norm, mla_out_norm, w_o,
           pre_mix_norm, post_mix_norm, pre_ffn_norm, post_ffn_norm, w_gate, w_up, w_down):
    b, seq, _ = x.shape
    depth = w_in.shape[0]
    l_real = N_META + seq
    l_pad = -(-l_real // SEQ_TILE) * SEQ_TILE
    assert l_pad % IN_ROWS == 0 and (b * l_pad) % POST_ROWS == 0
    meta = jnp.broadcast_to(meta_tokens.astype(x.dtype)[None], (b, N_META, D_MODEL))
    h = jnp.concatenate([meta, x, jnp.zeros((b, l_pad - l_real, D_MODEL), x.dtype)], axis=1)
    h = h.reshape(b * l_pad, D_MODEL)
    tab = _rope_table(l_pad)
    tri = _triangular()
    v_one = (jnp.arange(MLA_HEADS * MLA_VROWS) % MLA_VROWS >= MLA_V).astype(_F32).reshape(1, -1)
    o1 = 3 * SB_WIDTH
    o2 = o1 + Q_LORA
    o3 = o2 + KV_LORA
    n_chunks = D_FF // FF_CHUNK
    gain = lambda g: g.reshape(1, -1).astype(_F32)

    for layer in range(depth):
        wl = w_in[layer]
        w_sb = wl[:, :o1].astype(_BF16)
        zc = lambda w: jnp.zeros((D_MODEL, w), wl.dtype)
        w_lat = jnp.concatenate([wl[:, o1:o3], zc(MLA_NOPE), wl[:, o3:], zc(LANE - MLA_NOPE - MLA_ROPE)],
                                axis=1).astype(_BF16)
        wq = w_uq[layer].reshape(Q_LORA, MLA_HEADS, MLA_NOPE + MLA_ROPE)
        half = MLA_ROPE // 2
        wq_half = jnp.concatenate([jnp.zeros_like(wq[:, :, :MLA_NOPE]), -wq[:, :, MLA_NOPE + half:],
                                   wq[:, :, MLA_NOPE:MLA_NOPE + half]], axis=2)
        slot_pad = ((0, 0), (0, 0), (0, MLA_SLOT - MLA_NOPE - MLA_ROPE))
        wq_half = jnp.pad(wq_half, slot_pad).reshape(Q_LORA, MLA_HEADS * MLA_SLOT).astype(_BF16)
        wq = jnp.pad(wq, slot_pad).reshape(Q_LORA, MLA_HEADS * MLA_SLOT).astype(_BF16)
        wkv = w_ukv[layer].reshape(KV_LORA, MLA_HEADS, MLA_NOPE + MLA_V)
        wk = jnp.pad(wkv[:, :, :MLA_NOPE], ((0, 0), (0, 0), (0, MLA_SLOT - MLA_NOPE)))
        wk = wk.reshape(KV_LORA, MLA_HEADS * MLA_SLOT).astype(_BF16)
        wv = jnp.pad(wkv[:, :, MLA_NOPE:], ((0, 0), (0, 0), (0, MLA_VROWS - MLA_V)))
        wv = wv.reshape(KV_LORA, MLA_HEADS * MLA_VROWS).astype(_BF16)

        q_sb, k_sb, v_sb, q_m, k_m, v_m = _in_proj(
            h, tab, gain(pre_mix_norm[layer]), w_sb, w_lat, gain(q_lat_norm[layer]), gain(kv_lat_norm[layer]),
            wq, wq_half, wk, wv, v_one, l_pad)
        seq3 = lambda a: a.reshape(b, l_pad, a.shape[-1])
        o_sb = _sb_attention(seq3(q_sb), seq3(k_sb), _key_blocks_transposed(v_sb, b, l_pad, SB_KEYS), tri)
        o_mla = _mla_attention(seq3(q_m), seq3(k_m), _key_blocks_transposed(v_m, b, l_pad, SEQ_TILE))

        wg = w_gate[layer].reshape(D_MODEL, n_chunks, FF_CHUNK).transpose(1, 0, 2).astype(_BF16)
        wu = w_up[layer].reshape(D_MODEL, n_chunks, FF_CHUNK).transpose(1, 0, 2).astype(_BF16)
        wd = w_down[layer].reshape(n_chunks, FF_CHUNK, D_MODEL).astype(_BF16)
        h = _post(h, o_sb.reshape(b * l_pad, SB_WIDTH), o_mla.reshape(b * l_pad, MLA_WIDTH),
                  gain(sb_out_norm[layer]), gain(mla_out_norm[layer]), w_o[layer].astype(_BF16),
                  gain(post_mix_norm[layer]), gain(pre_ffn_norm[layer]), wg, wu, wd, gain(post_ffn_norm[layer]),
                  real_only=(b, l_pad, N_META, seq) if layer == depth - 1 else None)

    return h.reshape(b, seq, D_MODEL)
```

```python
import functools
import math

import jax
import jax.numpy as jnp
from jax import lax
from jax.experimental import pallas as pl
from jax.experimental.pallas import tpu as pltpu

D_MODEL = 1024
N_META = 16
SB_HEADS = 8
SB_HEAD_DIM = 64
SB_WIDTH = SB_HEADS * SB_HEAD_DIM
MLA_HEADS = 8
MLA_NOPE = 64
MLA_ROPE = 32
MLA_V = 64
Q_LORA = 384
KV_LORA = 256
MLA_WIDTH = MLA_HEADS * MLA_V
D_FF = 2816
ROPE_THETA = 10000.0
EPS = 1e-6

LANE = 128
MLA_SLOT = LANE
SEQ_TILE = 256
IN_ROWS = 544
POST_ROWS = 512
FF_CHUNK = 256
SB_KEYS = 128
SB_TILE = 256
MLA_VROWS = MLA_V + 16
SB_GROUP = 4
MLA_GROUP = 4
NEG_BIG = -1e30
LOG2E = math.log2(math.e)
UNDERFLOW_LOG2 = -150.0

_NT = (((1,), (1,)), ((), ()))
_F32 = jnp.float32
_BF16 = jnp.bfloat16


def _rms(x, gain):
    ms = jnp.mean(x * x, axis=-1, keepdims=True)
    return x * lax.rsqrt(ms + EPS) * gain


def _dot(a, b):
    return jnp.dot(a, b, preferred_element_type=_F32)


def _rope(x, cos, sin_lo, sin_hi):
    return x * cos + pltpu.roll(x, LANE - MLA_ROPE // 2, 1) * sin_lo + pltpu.roll(x, MLA_ROPE // 2, 1) * sin_hi


def _in_proj_kernel(h_ref, tab_ref, g_pre_ref, w_sb_ref, w_lat_ref, g_q_ref, g_kv_ref, wq_ref, wq_half_ref, wk_ref,
                    wv_ref, v_one_ref, qsb_ref, ksb_ref, vsb_ref, qm_ref, km_ref, vm_ref):
    u = _rms(h_ref[...], g_pre_ref[...]).astype(_BF16)
    p_sb = _dot(u, w_sb_ref[...])
    qsb_ref[...] = (p_sb[:, :SB_WIDTH] * (-LOG2E / math.sqrt(SB_HEAD_DIM))).astype(_BF16)
    ksb_ref[...] = p_sb[:, SB_WIDTH:2 * SB_WIDTH].astype(_BF16)
    vsb_ref[...] = p_sb[:, 2 * SB_WIDTH:].astype(_BF16)

    p_lat = _dot(u, w_lat_ref[...])
    c_q = _rms(p_lat[:, :Q_LORA], g_q_ref[...]).astype(_BF16)
    c_kv = _rms(p_lat[:, Q_LORA:Q_LORA + KV_LORA], g_kv_ref[...]).astype(_BF16)
    k_rope = p_lat[:, Q_LORA + KV_LORA:]

    tab = tab_ref[...]
    cos_q, sin_q, cos_k, slo_k, shi_k = [tab[:, i * LANE:(i + 1) * LANE] for i in range(5)]
    k_rope = _rope(k_rope, cos_k, slo_k, shi_k)

    q = _dot(c_q, wq_ref[...])
    q_half = _dot(c_q, wq_half_ref[...])
    k_nope = _dot(c_kv, wk_ref[...])
    for hh in range(MLA_HEADS):
        sl = slice(hh * MLA_SLOT, (hh + 1) * MLA_SLOT)
        qm_ref[:, sl] = (q[:, sl] * cos_q + q_half[:, sl] * sin_q).astype(_BF16)
        km_ref[:, sl] = (k_nope[:, sl] + k_rope).astype(_BF16)
    vm_ref[...] = (_dot(c_kv, wv_ref[...]) + v_one_ref[...]).astype(_BF16)


def _const_spec(shape):
    nd = len(shape)
    return pl.BlockSpec(shape, lambda *_: (0,) * nd, pipeline_mode=pl.Buffered(1))


def _in_proj(h2d, tab, g_pre, w_sb, w_lat, g_q, g_kv, wq, wq_half, wk, wv, v_one, l_pad):
    m = h2d.shape[0]
    t = IN_ROWS
    tiles_per_seq = l_pad // t
    row = lambda w: pl.BlockSpec((t, w), lambda i: (i, 0))
    out_widths = (SB_WIDTH, SB_WIDTH, SB_WIDTH, MLA_HEADS * MLA_SLOT, MLA_HEADS * MLA_SLOT, MLA_HEADS * MLA_VROWS)
    return pl.pallas_call(
        _in_proj_kernel,
        grid=(m // t,),
        in_specs=[row(D_MODEL),
                  pl.BlockSpec((t, tab.shape[1]), lambda i: (i % tiles_per_seq, 0)),
                  _const_spec(g_pre.shape), _const_spec(w_sb.shape), _const_spec(w_lat.shape),
                  _const_spec(g_q.shape), _const_spec(g_kv.shape),
                  _const_spec(wq.shape), _const_spec(wq_half.shape), _const_spec(wk.shape), _const_spec(wv.shape),
                  _const_spec(v_one.shape)],
        out_specs=[row(w) for w in out_widths],
        out_shape=[jax.ShapeDtypeStruct((m, w), _BF16) for w in out_widths],
        compiler_params=pltpu.CompilerParams(dimension_semantics=("arbitrary",),
                                             vmem_limit_bytes=52 * 1024 * 1024),
        name="in_proj",
    )(h2d, tab, g_pre, w_sb, w_lat, g_q, g_kv, wq, wq_half, wk, wv, v_one)


def _sb_kernel(q_ref, k_ref, vt_ref, tri_ref, o_ref, qh_ref, c_ref, acc_ref):
    tq, tk = SB_TILE, SB_KEYS
    qi = pl.program_id(1)
    low_head = lax.broadcasted_iota(jnp.int32, (tq, LANE), 1) < SB_HEAD_DIM
    for hp in range(SB_WIDTH // LANE):
        qp = q_ref[0, :, hp * LANE:(hp + 1) * LANE]
        zero = jnp.zeros_like(qp)
        qh_ref[2 * hp] = jnp.where(low_head, qp, zero)
        qh_ref[2 * hp + 1] = jnp.where(low_head, zero, qp)
    tri = tri_ref[...]
    c_ref[...] = jnp.zeros_like(c_ref)
    acc_ref[...] = jnp.zeros_like(acc_ref)

    def block(j, masked, q_lo=0, q_hi=tq):
        start = pl.multiple_of(j * tk, tk)
        nq = q_hi - q_lo
        qs = slice(q_lo, q_hi)
        if masked:
            key = start + lax.broadcasted_iota(jnp.int32, (tk, nq), 0)
            qry = qi * tq + q_lo + lax.broadcasted_iota(jnp.int32, (tk, nq), 1)
            valid = key < qry

        def scores(h):
            hp = h // 2
            kb = k_ref[0, pl.ds(start, tk), hp * LANE:(hp + 1) * LANE]
            return lax.dot_general(kb, qh_ref[h, qs, :], _NT, preferred_element_type=_F32)

        def cumsum(zn):
            low = jnp.minimum(zn, 0.0)
            lm = low - jnp.log2(1.0 + jnp.exp2((low - zn) + low))
            ls = lm - zn
            if masked:
                lm = jnp.where(valid, lm, 0.0)
            lm = lm.astype(_BF16)
            cs = _dot(tri, lm)
            return ls, cs, cs[0:1, :] + lm[0:1, :].astype(_F32)

        def accumulate(h, ls, cs, total):
            a = jnp.exp2(ls + cs + c_ref[h:h + 1, qs])
            if masked:
                a = jnp.where(valid, a, 0.0)
            rows = slice(h * SB_HEAD_DIM, (h + 1) * SB_HEAD_DIM)
            acc_ref[rows, qs] += _dot(vt_ref[0, j, rows, :], a.astype(_BF16))
            c_ref[h:h + 1, qs] += total

        groups = [range(g, g + SB_GROUP) for g in range(0, SB_HEADS, SB_GROUP)]
        zn = {h: scores(h) for h in groups[0]}
        st = {}
        for gi, group in enumerate(groups):
            if gi + 1 < len(groups):
                zn.update({h: scores(h) for h in groups[gi + 1]})
            st.update({h: cumsum(zn[h]) for h in group})
            if gi >= 1:
                for h in groups[gi - 1]:
                    accumulate(h, *st[h])
        for h in groups[-1]:
            accumulate(h, *st[h])

    per_tile = tq // tk
    for d in range(per_tile - 1, -1, -1):
        block(qi * per_tile + d, True, q_lo=d * tk)

    n_off = qi * per_tile

    def more(carry):
        i, live, _ = carry
        return jnp.logical_and(i < n_off, live > 0)

    def body(carry):
        i, _, late_live = carry
        j = n_off - 1 - i
        lax.cond(late_live > 0, lambda: block(j, False), lambda: block(j, False, q_hi=tq // 2))
        c = c_ref[...]
        live = (jnp.max(c) > UNDERFLOW_LOG2).astype(jnp.int32)
        return i + 1, live, (jnp.max(c[:, tq // 2:]) > UNDERFLOW_LOG2).astype(jnp.int32)

    lax.while_loop(more, body, (jnp.int32(0), jnp.int32(1), jnp.int32(1)))
    for hp in range(SB_WIDTH // LANE):
        o_ref[0, :, hp * LANE:(hp + 1) * LANE] = acc_ref[hp * LANE:(hp + 1) * LANE, :].T


def _sb_attention(q, k, vt, tri):
    b, l_pad, _ = q.shape
    tq = SB_TILE
    return pl.pallas_call(
        _sb_kernel,
        grid=(b, l_pad // tq),
        in_specs=[pl.BlockSpec((1, tq, SB_WIDTH), lambda bi, qi: (bi, qi, 0)),
                  pl.BlockSpec((1, l_pad, SB_WIDTH), lambda bi, qi: (bi, 0, 0)),
                  pl.BlockSpec((1,) + vt.shape[1:], lambda bi, qi: (bi, 0, 0, 0)),
                  pl.BlockSpec(tri.shape, lambda bi, qi: (0, 0))],
        out_specs=pl.BlockSpec((1, tq, SB_WIDTH), lambda bi, qi: (bi, qi, 0)),
        out_shape=jax.ShapeDtypeStruct((b, l_pad, SB_WIDTH), _F32),
        scratch_shapes=[pltpu.VMEM((SB_HEADS, tq, LANE), _BF16),
                        pltpu.VMEM((SB_HEADS, tq), _F32),
                        pltpu.VMEM((SB_WIDTH, tq), _F32)],
        compiler_params=pltpu.CompilerParams(dimension_semantics=("arbitrary",) * 2,
                                             vmem_limit_bytes=40 * 1024 * 1024),
        name="sb_attention",
    )(q, k, vt, tri)


def _mla_kernel(q_ref, k_ref, vt_ref, o_ref, s_ref, m_ref, acc_ref):
    tq = tk = SEQ_TILE
    qi = pl.program_id(1)
    m_ref[...] = jnp.full_like(m_ref, NEG_BIG)
    acc_ref[...] = jnp.zeros_like(acc_ref)
    lead = range(MLA_GROUP)
    rest = range(MLA_GROUP, MLA_HEADS)

    def scores(j, h):
        sl = slice(h * MLA_SLOT, (h + 1) * MLA_SLOT)
        start = pl.multiple_of(j * tk, tk)
        return lax.dot_general(k_ref[0, pl.ds(start, tk), sl], q_ref[0, :, sl], _NT,
                               preferred_element_type=_F32)

    def lead_scores(j):
        for h in lead:
            s_ref[h] = scores(j, h)

    def block(j, j_next, masked):
        if masked:
            valid = lax.broadcasted_iota(jnp.int32, (tk, tq), 0) <= lax.broadcasted_iota(jnp.int32, (tk, tq), 1)

        def softmax(h, s):
            if masked:
                s = jnp.where(valid, s, NEG_BIG)
            m_old = m_ref[h:h + 1, :]
            m_new = jnp.maximum(m_old, jnp.max(s, axis=0, keepdims=True))
            alpha = jnp.exp2(m_old - m_new)
            p = jnp.exp2(s - m_new)
            m_ref[h:h + 1, :] = m_new
            return alpha, p.astype(_BF16)

        def weighted_values(h, alpha, p):
            rows = slice(h * MLA_VROWS, (h + 1) * MLA_VROWS)
            acc_ref[rows, :] = alpha * acc_ref[rows, :] + _dot(vt_ref[0, j, rows, :], p)

        s_rest = [scores(j, h) for h in rest]
        probs = [softmax(h, s_ref[h]) for h in lead]
        for h, pr in zip(lead, probs):
            weighted_values(h, *pr)
        if j_next is not None:
            lead_scores(j_next)
        probs = [softmax(h, s) for h, s in zip(rest, s_rest)]
        for h, pr in zip(rest, probs):
            weighted_values(h, *pr)

    def two_blocks(i, carry):
        block(2 * i, 2 * i + 1, False)
        block(2 * i + 1, 2 * i + 2, False)
        return carry

    lead_scores(0)
    lax.fori_loop(0, qi // 2, two_blocks, 0)

    @pl.when(qi % 2 == 1)
    def _():
        block(qi - 1, qi, False)

    block(qi, None, True)

    for hp in range(MLA_WIDTH // LANE):
        pair = []
        for h in (2 * hp, 2 * hp + 1):
            base = h * MLA_VROWS
            pair.append(acc_ref[base:base + MLA_V, :] / acc_ref[base + MLA_V:base + MLA_V + 1, :])
        o_ref[0, :, hp * LANE:(hp + 1) * LANE] = jnp.concatenate(pair, axis=0).T


def _mla_attention(q, k, vt):
    b, l_pad, _ = q.shape
    tq = SEQ_TILE
    width = MLA_HEADS * MLA_SLOT
    return pl.pallas_call(
        _mla_kernel,
        grid=(b, l_pad // tq),
        in_specs=[pl.BlockSpec((1, tq, width), lambda bi, qi: (bi, qi, 0)),
                  pl.BlockSpec((1, l_pad, width), lambda bi, qi: (bi, 0, 0)),
                  pl.BlockSpec((1,) + vt.shape[1:], lambda bi, qi: (bi, 0, 0, 0))],
        out_specs=pl.BlockSpec((1, tq, MLA_WIDTH), lambda bi, qi: (bi, qi, 0)),
        out_shape=jax.ShapeDtypeStruct((b, l_pad, MLA_WIDTH), _F32),
        scratch_shapes=[pltpu.VMEM((MLA_GROUP, tq, tq), _F32),
                        pltpu.VMEM((MLA_HEADS, tq), _F32),
                        pltpu.VMEM((MLA_HEADS * MLA_VROWS, tq), _F32)],
        compiler_params=pltpu.CompilerParams(dimension_semantics=("arbitrary",) * 2,
                                             vmem_limit_bytes=48 * 1024 * 1024),
        name="mla_attention",
    )(q, k, vt)


def _post_kernel(h_ref, osb_ref, omla_ref, g_sb_ref, g_mla_ref, wo_ref, g_post_ref, g_ffn_ref,
                 wg_ref, wu_ref, wd_ref, g_out_ref, out_ref, f_ref, acc_ref):
    a = _rms(osb_ref[...], g_sb_ref[...]).astype(_BF16)
    b = _rms(omla_ref[...], g_mla_ref[...]).astype(_BF16)
    mix = _dot(a, wo_ref[:SB_WIDTH, :]) + _dot(b, wo_ref[SB_WIDTH:, :])
    h1 = h_ref[...] + _rms(mix, g_post_ref[...])
    out_ref[...] = h1
    f_ref[...] = _rms(h1, g_ffn_ref[...]).astype(_BF16)
    acc_ref[...] = jnp.zeros_like(acc_ref)

    def chunk(c, carry):
        f = f_ref[...]
        g = _dot(f, wg_ref[c])
        up = _dot(f, wu_ref[c])
        act = g * (1.0 / (1.0 + jnp.exp(-g))) * up
        acc_ref[...] += _dot(act.astype(_BF16), wd_ref[c])
        return carry

    lax.fori_loop(0, D_FF // FF_CHUNK, chunk, 0, unroll=True)
    out_ref[...] += _rms(acc_ref[...], g_out_ref[...])


def _post(h2d, o_sb, o_mla, g_sb, g_mla, wo, g_post, g_ffn, wg, wu, wd, g_out, real_only=None):
    m = h2d.shape[0]
    t = POST_ROWS
    if real_only is None:
        grid = (m // t,)
        row = lambda w: pl.BlockSpec((t, w), lambda i: (i, 0))
        out_spec, out_rows = row(D_MODEL), m
    else:
        b, l_pad, first, n_rows = real_only
        assert n_rows % t == 0
        tiles = n_rows // t
        grid = (b, tiles)
        row = lambda w: pl.BlockSpec((pl.Element(t), pl.Element(w)),
                                     lambda bi, ti: (pl.multiple_of(bi * l_pad + first + ti * t, 8), 0))
        out_spec, out_rows = pl.BlockSpec((t, D_MODEL), lambda bi, ti: (bi * tiles + ti, 0)), b * n_rows
    consts = (g_sb, g_mla, wo, g_post, g_ffn, wg, wu, wd, g_out)
    return pl.pallas_call(
        _post_kernel,
        grid=grid,
        in_specs=[row(D_MODEL), row(SB_WIDTH), row(MLA_WIDTH)] + [_const_spec(c.shape) for c in consts],
        out_specs=out_spec,
        out_shape=jax.ShapeDtypeStruct((out_rows, D_MODEL), _F32),
        scratch_shapes=[pltpu.VMEM((t, D_MODEL), _BF16), pltpu.VMEM((t, D_MODEL), _F32)],
        compiler_params=pltpu.CompilerParams(dimension_semantics=("arbitrary",) * len(grid),
                                             vmem_limit_bytes=56 * 1024 * 1024),
        name="post_mix_ffn",
    )(h2d, o_sb, o_mla, *consts)


def _rope_table(l_pad):
    half = MLA_ROPE // 2
    inv_freq = 1.0 / (ROPE_THETA ** (jnp.arange(0, MLA_ROPE, 2, dtype=_F32) / MLA_ROPE))
    ang = jnp.arange(l_pad, dtype=_F32)[:, None] * inv_freq[None, :]
    cos, sin = jnp.cos(ang), jnp.sin(ang)
    z = lambda w: jnp.zeros((l_pad, w), _F32)
    tail = LANE - MLA_NOPE - MLA_ROPE
    cos_t = jnp.concatenate([jnp.ones((l_pad, MLA_NOPE), _F32), cos, cos, z(tail)], axis=1)
    sin_lo = jnp.concatenate([z(MLA_NOPE), -sin, z(half), z(tail)], axis=1)
    sin_hi = jnp.concatenate([z(MLA_NOPE), z(half), sin, z(tail)], axis=1)
    scale = LOG2E / math.sqrt(MLA_NOPE + MLA_ROPE)
    return jnp.concatenate([cos_t * scale, (sin_hi - sin_lo) * scale, cos_t, sin_lo, sin_hi], axis=1)


def _triangular():
    s = jnp.arange(SB_KEYS)[:, None]
    j = jnp.arange(SB_KEYS)[None, :]
    return (j > s).astype(_BF16)


def _key_blocks_transposed(v, b, l_pad, tk):
    return v.reshape(b, l_pad // tk, tk, v.shape[-1]).transpose(0, 1, 3, 2)


def kernel(x, meta_tokens, w_in, q_lat_norm, kv_lat_norm, w_uq, w_ukv, sb_out_norm, mla_out_norm, w_o,
           pre_mix_norm, post_mix_norm, pre_ffn_norm, post_ffn_norm, w_gate, w_up, w_down):
    b, seq, _ = x.shape
    depth = w_in.shape[0]
    l_real = N_META + seq
    l_pad = -(-l_real // SEQ_TILE) * SEQ_TILE
    assert l_pad % IN_ROWS == 0 and (b * l_pad) % POST_ROWS == 0
    meta = jnp.broadcast_to(meta_tokens.astype(x.dtype)[None], (b, N_META, D_MODEL))
    h = jnp.concatenate([meta, x, jnp.zeros((b, l_pad - l_real, D_MODEL), x.dtype)], axis=1)
    h = h.reshape(b * l_pad, D_MODEL)
    tab = _rope_table(l_pad)
    tri = _triangular()
    v_one = (jnp.arange(MLA_HEADS * MLA_VROWS) % MLA_VROWS >= MLA_V).astype(_F32).reshape(1, -1)
    o1 = 3 * SB_WIDTH
    o2 = o1 + Q_LORA
    o3 = o2 + KV_LORA
    n_chunks = D_FF // FF_CHUNK
    gain = lambda g: g.reshape(1, -1).astype(_F32)

    for layer in range(depth):
        wl = w_in[layer]
        w_sb = wl[:, :o1].astype(_BF16)
        zc = lambda w: jnp.zeros((D_MODEL, w), wl.dtype)
        w_lat = jnp.concatenate([wl[:, o1:o3], zc(MLA_NOPE), wl[:, o3:], zc(LANE - MLA_NOPE - MLA_ROPE)],
                                axis=1).astype(_BF16)
        wq = w_uq[layer].reshape(Q_LORA, MLA_HEADS, MLA_NOPE + MLA_ROPE)
        half = MLA_ROPE // 2
        wq_half = jnp.concatenate([jnp.zeros_like(wq[:, :, :MLA_NOPE]), -wq[:, :, MLA_NOPE + half:],
                                   wq[:, :, MLA_NOPE:MLA_NOPE + half]], axis=2)
        slot_pad = ((0, 0), (0, 0), (0, MLA_SLOT - MLA_NOPE - MLA_ROPE))
        wq_half = jnp.pad(wq_half, slot_pad).reshape(Q_LORA, MLA_HEADS * MLA_SLOT).astype(_BF16)
        wq = jnp.pad(wq, slot_pad).reshape(Q_LORA, MLA_HEADS * MLA_SLOT).astype(_BF16)
        wkv = w_ukv[layer].reshape(KV_LORA, MLA_HEADS, MLA_NOPE + MLA_V)
        wk = jnp.pad(wkv[:, :, :MLA_NOPE], ((0, 0), (0, 0), (0, MLA_SLOT - MLA_NOPE)))
        wk = wk.reshape(KV_LORA, MLA_HEADS * MLA_SLOT).astype(_BF16)
        wv = jnp.pad(wkv[:, :, MLA_NOPE:], ((0, 0), (0, 0), (0, MLA_VROWS - MLA_V)))
        wv = wv.reshape(KV_LORA, MLA_HEADS * MLA_VROWS).astype(_BF16)

        q_sb, k_sb, v_sb, q_m, k_m, v_m = _in_proj(
            h, tab, gain(pre_mix_norm[layer]), w_sb, w_lat, gain(q_lat_norm[layer]), gain(kv_lat_norm[layer]),
            wq, wq_half, wk, wv, v_one, l_pad)
        seq3 = lambda a: a.reshape(b, l_pad, a.shape[-1])
        o_sb = _sb_attention(seq3(q_sb), seq3(k_sb), _key_blocks_transposed(v_sb, b, l_pad, SB_KEYS), tri)
        o_mla = _mla_attention(seq3(q_m), seq3(k_m), _key_blocks_transposed(v_m, b, l_pad, SEQ_TILE))

        wg = w_gate[layer].reshape(D_MODEL, n_chunks, FF_CHUNK).transpose(1, 0, 2).astype(_BF16)
        wu = w_up[layer].reshape(D_MODEL, n_chunks, FF_CHUNK).transpose(1, 0, 2).astype(_BF16)
        wd = w_down[layer].reshape(n_chunks, FF_CHUNK, D_MODEL).astype(_BF16)
        h = _post(h, o_sb.reshape(b * l_pad, SB_WIDTH), o_mla.reshape(b * l_pad, MLA_WIDTH),
                  gain(sb_out_norm[layer]), gain(mla_out_norm[layer]), w_o[layer].astype(_BF16),
                  gain(post_mix_norm[layer]), gain(pre_ffn_norm[layer]), wg, wu, wd, gain(post_ffn_norm[layer]),
                  real_only=(b, l_pad, N_META, seq) if layer == depth - 1 else None)

    return h.reshape(b, seq, D_MODEL)
```
